```python
import math
import jax, jax.numpy as jnp
from jax import lax
import numpy as np

D_MODEL = 2048
BATCH = 2
SEQ = 4096
DEPTH = 1

HEAD_DIM = 128
N_HEADS_DIL = 8
D_DIL = N_HEADS_DIL * HEAD_DIM
DIL_PATTERNS = ((128, 1), (512, 4), (2048, 16))
BLOCK = 128
N_GROUPS_SG = 4
SG_CHUNK = 128
SG_GROUP_DIM = 128
D_SG = N_GROUPS_SG * SG_GROUP_DIM
N_HEADS_MEM = 4
D_MEM_ATTN = N_HEADS_MEM * HEAD_DIM
N_MEM = 256
D_MIX = D_DIL + D_SG + D_MEM_ATTN
D_IN_PROJ = 3 * D_DIL + 2 * D_SG + D_MEM_ATTN
N_BUCKETS = 32
MAX_DISTANCE = 2048
D_FF = -(-(8 * D_MODEL) // (3 * 256)) * 256
DEEPNORM_ALPHA = (2 * DEPTH) ** 0.25
DEEPNORM_BETA = (8 * DEPTH) ** -0.25
LN_EPS = 1e-5

kernel_name = "hybrid_dilated_sgmlp_memxattn_deepnorm_block"


def layer_norm(x, g, b):
    xf = x.astype(jnp.float32)
    mu = xf.mean(-1, keepdims=True)
    var = jnp.mean(jnp.square(xf - mu), -1, keepdims=True)
    return ((xf - mu) * lax.rsqrt(var + LN_EPS) * g + b).astype(x.dtype)


def t5_bucket(dist):
    max_exact = N_BUCKETS // 2
    d = jnp.maximum(dist, 1).astype(jnp.float32)
    large = max_exact + (jnp.log(d / max_exact) / math.log(MAX_DISTANCE / max_exact)
                         * (N_BUCKETS - max_exact)).astype(jnp.int32)
    large = jnp.minimum(large, N_BUCKETS - 1)
    return jnp.where(dist < max_exact, dist, large)


def dilated_pattern_attn(q, k, v, rel_bias, dilation, n_steps):
    b, h, s, dh = q.shape
    L = s // dilation

    def split(t):
        return t.reshape(b, h, L, dilation, dh).transpose(0, 1, 3, 2, 4)

    qs, ks, vs = split(q), split(k), split(v)
    nb = -(-L // BLOCK)
    lp = nb * BLOCK
    qs = jnp.pad(qs, ((0, 0), (0, 0), (0, 0), (0, lp - L), (0, 0)))
    pad_kv = ((0, 0), (0, 0), (0, 0), (n_steps, lp - L), (0, 0))
    ks = jnp.pad(ks, pad_kv)
    vs = jnp.pad(vs, pad_kv)
    key_idx = jnp.arange(nb)[:, None] * BLOCK + jnp.arange(BLOCK + n_steps)[None, :]
    kb = ks[:, :, :, key_idx]
    vb = vs[:, :, :, key_idx]
    qb = qs.reshape(b, h, dilation, nb, BLOCK, dh)

    steps = jnp.arange(BLOCK)[:, None] + n_steps - jnp.arange(BLOCK + n_steps)[None, :]
    key_sub = key_idx - n_steps
    valid = (steps >= 0) & (steps <= n_steps) & (key_sub[:, None, :] >= 0)
    bias = rel_bias[t5_bucket(jnp.maximum(steps, 0) * dilation)].transpose(2, 0, 1)

    scores = jnp.einsum('bhrnqd,bhrnkd->bhrnqk', qb, kb).astype(jnp.float32) * (dh ** -0.5)
    scores = scores + bias[None, :, None, None].astype(jnp.float32)
    scores = jnp.where(valid[None, None, None], scores, jnp.finfo(jnp.float32).min)
    m = scores.max(-1, keepdims=True)
    e = jnp.exp(scores - m)
    den = e.sum(-1)
    out = jnp.einsum('bhrnqk,bhrnkd->bhrnqd', e, vb.astype(jnp.float32)) / den[..., None]
    lse = m[..., 0] + jnp.log(den)

    out = out.reshape(b, h, dilation, lp, dh)[:, :, :, :L].transpose(0, 1, 3, 2, 4).reshape(b, h, s, dh)
    lse = lse.reshape(b, h, dilation, lp)[:, :, :, :L].transpose(0, 1, 3, 2).reshape(b, h, s)
    return out, lse


def dilated_mixture_attn(q, k, v, rel_bias):
    outs, lses = [], []
    for window, dilation in DIL_PATTERNS:
        o, l = dilated_pattern_attn(q, k, v, rel_bias, dilation, window // dilation)
        outs.append(o)
        lses.append(l)
    w = jax.nn.softmax(jnp.stack(lses, 0), axis=0)
    return jnp.sum(w[..., None] * jnp.stack(outs, 0), axis=0)


def spatial_gating(u, v, ln_g, ln_b, w_s, b_s):
    v = layer_norm(v, ln_g, ln_b)
    b, s, _ = v.shape
    vc = v.reshape(b, s // SG_CHUNK, SG_CHUNK, N_GROUPS_SG, SG_GROUP_DIM)
    mask = jnp.tril(jnp.ones((SG_CHUNK, SG_CHUNK), dtype=bool))
    w = jnp.where(mask[None], w_s, 0)
    mixed = jnp.einsum('gij,bcjgd->bcigd', w, vc) + b_s.T[None, None, :, :, None]
    return u * mixed.reshape(b, s, D_SG)


def memory_cross_attn(q_m, mem, w_kv):
    b, s, _ = q_m.shape
    kv = mem @ w_kv
    k_m, v_m = jnp.split(kv, 2, axis=-1)
    n_mem = mem.shape[1]
    qh = q_m.reshape(b, s, N_HEADS_MEM, HEAD_DIM)
    kh = k_m.reshape(b, n_mem, N_HEADS_MEM, HEAD_DIM)
    vh = v_m.reshape(b, n_mem, N_HEADS_MEM, HEAD_DIM)
    scores = jnp.einsum('bshd,bmhd->bhsm', qh, kh).astype(jnp.float32) * (HEAD_DIM ** -0.5)
    p = jax.nn.softmax(scores, axis=-1)
    o = jnp.einsum('bhsm,bmhd->bshd', p, vh.astype(jnp.float32))
    return o.reshape(b, s, D_MEM_ATTN)


def setup_inputs(seed: int = 0) -> dict:
    key = jax.random.key(seed)
    ks = jax.random.split(key, 20)
    n = jax.random.normal
    f32 = jnp.float32
    return {
        "x": n(ks[0], (BATCH, SEQ, D_MODEL), f32),
        "mem": n(ks[1], (BATCH, N_MEM, D_MODEL), f32),
        "w_in": n(ks[2], (DEPTH, D_MODEL, D_IN_PROJ), f32) * D_MODEL ** -0.5,
        "rel_bias": n(ks[3], (N_BUCKETS, N_HEADS_DIL), f32) * 0.5,
        "sg_ln_g": 1.0 + 0.02 * n(ks[4], (DEPTH, D_SG), f32),
        "sg_ln_b": 0.02 * n(ks[5], (DEPTH, D_SG), f32),
        "w_spatial": n(ks[6], (DEPTH, N_GROUPS_SG, SG_CHUNK, SG_CHUNK), f32) * SG_CHUNK ** -0.5,
        "b_spatial": 1.0 + 0.02 * n(ks[7], (DEPTH, N_GROUPS_SG, SG_CHUNK), f32),
        "w_mem_kv": n(ks[8], (DEPTH, D_MODEL, 2 * D_MEM_ATTN), f32) * D_MODEL ** -0.5,
        "w_out": n(ks[9], (DEPTH, D_MIX, D_MODEL), f32) * (D_MIX ** -0.5 * DEEPNORM_BETA),
        "ln1_g": 1.0 + 0.02 * n(ks[10], (DEPTH, D_MODEL), f32),
        "ln1_b": 0.02 * n(ks[11], (DEPTH, D_MODEL), f32),
        "w_gate": n(ks[12], (DEPTH, D_MODEL, D_FF), f32) * D_MODEL ** -0.5,
        "w_up": n(ks[13], (DEPTH, D_MODEL, D_FF), f32) * D_MODEL ** -0.5,
        "w_down": n(ks[14], (DEPTH, D_FF, D_MODEL), f32) * (D_FF ** -0.5 * DEEPNORM_BETA),
        "ln2_g": 1.0 + 0.02 * n(ks[15], (DEPTH, D_MODEL), f32),
        "ln2_b": 0.02 * n(ks[16], (DEPTH, D_MODEL), f32),
    }


def reference(x, mem, w_in, rel_bias, sg_ln_g, sg_ln_b, w_spatial, b_spatial, w_mem_kv, w_out,
              ln1_g, ln1_b, w_gate, w_up, w_down, ln2_g, ln2_b):
    b, s, _ = x.shape
    cuts = [D_DIL, 2 * D_DIL, 3 * D_DIL, 3 * D_DIL + D_SG, 3 * D_DIL + 2 * D_SG]
    for l in range(DEPTH):
        hcat = x @ w_in[l]
        q_d, k_d, v_d, u_sg, v_sg, q_m = jnp.split(hcat, cuts, axis=-1)

        def heads(t):
            return t.reshape(b, s, N_HEADS_DIL, HEAD_DIM).transpose(0, 2, 1, 3)

        o_dil = dilated_mixture_attn(heads(q_d), heads(k_d), heads(v_d), rel_bias)
        o_dil = o_dil.transpose(0, 2, 1, 3).reshape(b, s, D_DIL).astype(x.dtype)

        o_sg = spatial_gating(jax.nn.gelu(u_sg), jax.nn.gelu(v_sg), sg_ln_g[l], sg_ln_b[l],
                              w_spatial[l], b_spatial[l]).astype(x.dtype)

        o_mem = memory_cross_attn(q_m, mem, w_mem_kv[l]).astype(x.dtype)

        mix = jnp.concatenate([o_dil, o_sg, o_mem], axis=-1) @ w_out[l]
        x = layer_norm(DEEPNORM_ALPHA * x + mix, ln1_g[l], ln1_b[l])

        f = (jax.nn.silu(x @ w_gate[l]) * (x @ w_up[l])) @ w_down[l]
        x = layer_norm(DEEPNORM_ALPHA * x + f, ln2_g[l], ln2_b[l])
    return x
```

```python
import functools
import math

import jax
import jax.numpy as jnp
from jax import lax
from jax.experimental import pallas as pl
from jax.experimental.pallas import tpu as pltpu

D_MODEL = 2048
HEAD_DIM = 128
N_HEADS_DIL = 8
D_DIL = N_HEADS_DIL * HEAD_DIM
DILATIONS = (16, 4, 1)
N_STEPS = 128
BLOCK = 128
N_GROUPS_SG = 4
SG_CHUNK = 128
D_SG = N_GROUPS_SG * SG_CHUNK
N_HEADS_MEM = 4
D_MEM_ATTN = N_HEADS_MEM * HEAD_DIM
N_BUCKETS = 32
MAX_DISTANCE = 2048
DEEPNORM_ALPHA = 2.0 ** 0.25
LN_EPS = 1e-5
ATTN_SCALE = HEAD_DIM ** -0.5
MASK_VALUE = -1e30
MAX_DIL = 16
KV_PAD = 128

VMEM_LIMIT_BYTES = 56 * 1024 * 1024


def _layer_norm(y, g, b):
    mu = jnp.mean(y, axis=-1, keepdims=True)
    d = y - mu
    var = jnp.mean(d * d, axis=-1, keepdims=True)
    return d * lax.rsqrt(var + LN_EPS) * g + b


def _proj_kernel(x_ref, w_ref, o_ref, xb_ref):
    @pl.when(pl.program_id(1) == 0)
    def _():
        xb_ref[...] = x_ref[...].astype(jnp.bfloat16)

    o_ref[...] = jnp.dot(xb_ref[...], w_ref[...],
                         preferred_element_type=jnp.float32).astype(o_ref.dtype)


def _proj(x2d, w_bf16, bm, bn, out_dtype):
    m, k = x2d.shape
    n = w_bf16.shape[1]
    return pl.pallas_call(
        _proj_kernel,
        grid=(m // bm, n // bn),
        in_specs=[pl.BlockSpec((bm, k), lambda i, j: (i, 0)),
                  pl.BlockSpec((k, bn), lambda i, j: (0, j))],
        out_specs=pl.BlockSpec((bm, bn), lambda i, j: (i, j)),
        out_shape=jax.ShapeDtypeStruct((m, n), out_dtype),
        scratch_shapes=[pltpu.VMEM((bm, k), jnp.bfloat16)],
        compiler_params=pltpu.CompilerParams(
            dimension_semantics=("parallel", "arbitrary"),
            vmem_limit_bytes=VMEM_LIMIT_BYTES),
        name="proj",
    )(x2d, w_bf16)


def _t5_bucket(dist):
    max_exact = N_BUCKETS // 2
    d = jnp.maximum(dist, 1).astype(jnp.float32)
    large = max_exact + (jnp.log(d / max_exact) / math.log(MAX_DISTANCE / max_exact)
                         * (N_BUCKETS - max_exact)).astype(jnp.int32)
    large = jnp.minimum(large, N_BUCKETS - 1)
    return jnp.where(dist < max_exact, dist, large)


def _dil_attn_kernel(rb_ref, q_ref, k_ref, v_ref, o_ref,
                     qs_ref, ks_ref, vs_ref, acc_ref, m_ref, l_ref, bias_ref):
    seq = q_ref.shape[0]
    rows_per_class = seq // MAX_DIL
    h = pl.program_id(1)

    zeros_pad = jnp.zeros((KV_PAD, HEAD_DIM), jnp.float32)
    qs_ref[0:KV_PAD, :] = zeros_pad
    ks_ref[0:KV_PAD, :] = zeros_pad
    vs_ref[0:KV_PAD, :] = zeros_pad
    for c in range(MAX_DIL):
        dst = pl.ds(KV_PAD + c * rows_per_class, rows_per_class)
        src = pl.ds(c, rows_per_class, stride=MAX_DIL)
        qs_ref[dst, :] = q_ref[src, :] * ATTN_SCALE
        ks_ref[dst, :] = k_ref[src, :]
        vs_ref[dst, :] = v_ref[src, :]

    acc_ref[...] = jnp.zeros(acc_ref.shape, jnp.float32)
    l_ref[...] = jnp.zeros(l_ref.shape, jnp.float32)
    m_ref[...] = jnp.full(m_ref.shape, MASK_VALUE, jnp.float32)

    row = lax.broadcasted_iota(jnp.int32, (BLOCK, 2 * BLOCK), 0)
    col = lax.broadcasted_iota(jnp.int32, (BLOCK, 2 * BLOCK), 1)
    for p, dil in enumerate(DILATIONS):
        n_cls = MAX_DIL // dil
        q_rows = BLOCK // n_cls
        k_rows = 2 * BLOCK // n_cls
        q_sub = (row % q_rows) * n_cls + row // q_rows
        k_sub = (col % k_rows) * n_cls + col // k_rows
        steps = q_sub + N_STEPS - k_sub
        valid = (steps >= 0) & (steps <= N_STEPS)
        bucket = _t5_bucket(jnp.maximum(steps, 0) * dil)
        bias = jnp.zeros((BLOCK, 2 * BLOCK), jnp.float32)
        for b in range(N_BUCKETS):
            bias = jnp.where(bucket == b, rb_ref[b, h], bias)
        bias_ref[2 * p] = jnp.where(valid, bias, MASK_VALUE)
        bias_ref[2 * p + 1] = jnp.where(valid & (k_sub >= BLOCK), bias, MASK_VALUE)

    for p, dil in enumerate(DILATIONS):
        n_cls = MAX_DIL // dil
        q_rows = BLOCK // n_cls
        k_rows = 2 * BLOCK // n_cls
        blocks_per_sub = seq // dil // BLOCK

        def block_body(blk, carry, p=p, dil=dil, n_cls=n_cls, q_rows=q_rows, k_rows=k_rows,
                       blocks_per_sub=blocks_per_sub):
            c_sub = blk // blocks_per_sub
            n = blk % blocks_per_sub
            o_offs, q_offs, k_offs = [], [], []
            for mcls in range(n_cls):
                base = (c_sub + dil * mcls) * rows_per_class + n * q_rows
                o_offs.append(pl.multiple_of(base, 8))
                q_offs.append(pl.multiple_of(base + KV_PAD, 8))
                k_offs.append(pl.multiple_of(base + (KV_PAD - q_rows), 8))
            qb = jnp.concatenate([qs_ref[pl.ds(o, q_rows), :] for o in q_offs], axis=0)
            kb = jnp.concatenate([ks_ref[pl.ds(o, k_rows), :] for o in k_offs], axis=0)
            vb = jnp.concatenate([vs_ref[pl.ds(o, k_rows), :] for o in k_offs], axis=0)
            s = lax.dot_general(qb.astype(jnp.bfloat16), kb.astype(jnp.bfloat16),
                                (((1,), (1,)), ((), ())),
                                preferred_element_type=jnp.float32)
            s = s + bias_ref[2 * p + jnp.where(n == 0, 1, 0)]
            m_blk = jnp.max(s, axis=-1, keepdims=True)
            e = jnp.exp(s - m_blk)
            l_blk = jnp.sum(e, axis=-1, keepdims=True)
            pv = jnp.dot(e.astype(jnp.bfloat16), vb.astype(jnp.bfloat16),
                         preferred_element_type=jnp.float32)
            for mcls in range(n_cls):
                rows = pl.ds(o_offs[mcls], q_rows)
                sl = slice(mcls * q_rows, (mcls + 1) * q_rows)
                m_old = m_ref[rows, :]
                m_new = jnp.maximum(m_old, m_blk[sl])
                a_old = jnp.exp(m_old - m_new)
                a_blk = jnp.exp(m_blk[sl] - m_new)
                l_ref[rows, :] = l_ref[rows, :] * a_old + l_blk[sl] * a_blk
                acc_ref[rows, :] = acc_ref[rows, :] * a_old + pv[sl] * a_blk
                m_ref[rows, :] = m_new
            return carry

        lax.fori_loop(0, dil * blocks_per_sub, block_body, 0)

    for c in range(MAX_DIL):
        src = pl.ds(c * rows_per_class, rows_per_class)
        o_ref[pl.ds(c, rows_per_class, stride=MAX_DIL), :] = acc_ref[src, :] / l_ref[src, :]


def _dilated_attention(hcat, rel_bias, batch, seq):
    d_in = hcat.shape[-1]
    hcat3 = hcat.reshape(batch, seq, d_in)
    blk = (None, seq, HEAD_DIM)
    return pl.pallas_call(
        _dil_attn_kernel,
        grid=(batch, N_HEADS_DIL),
        in_specs=[pl.BlockSpec(memory_space=pltpu.SMEM),
                  pl.BlockSpec(blk, lambda b, h: (b, 0, h)),
                  pl.BlockSpec(blk, lambda b, h: (b, 0, N_HEADS_DIL + h)),
                  pl.BlockSpec(blk, lambda b, h: (b, 0, 2 * N_HEADS_DIL + h))],
        out_specs=pl.BlockSpec(blk, lambda b, h: (b, 0, h)),
        out_shape=jax.ShapeDtypeStruct((batch, seq, D_DIL), jnp.float32),
        scratch_shapes=[pltpu.VMEM((KV_PAD + seq, HEAD_DIM), jnp.float32),
                        pltpu.VMEM((KV_PAD + seq, HEAD_DIM), jnp.float32),
                        pltpu.VMEM((KV_PAD + seq, HEAD_DIM), jnp.float32),
                        pltpu.VMEM((seq, HEAD_DIM), jnp.float32),
                        pltpu.VMEM((seq, HEAD_DIM), jnp.float32),
                        pltpu.VMEM((seq, HEAD_DIM), jnp.float32),
                        pltpu.VMEM((2 * len(DILATIONS), BLOCK, 2 * BLOCK), jnp.float32)],
        compiler_params=pltpu.CompilerParams(
            dimension_semantics=("parallel", "parallel"),
            vmem_limit_bytes=VMEM_LIMIT_BYTES),
        name="dilated_attn",
    )(rel_bias, hcat3, hcat3, hcat3)


def _mix_kernel(x_ref, odil_ref, u_ref, v_ref, qm_ref, kv_ref, wsp_ref, bsp_ref,
                sgg_ref, sgb_ref, wout_ref, g1_ref, b1_ref, o_ref):
    tm = x_ref.shape[0]

    u = jax.nn.gelu(u_ref[...])
    v = _layer_norm(jax.nn.gelu(v_ref[...]), sgg_ref[...], sgb_ref[...]).astype(jnp.bfloat16)
    r_i = lax.broadcasted_iota(jnp.int32, (SG_CHUNK, SG_CHUNK), 0)
    c_i = lax.broadcasted_iota(jnp.int32, (SG_CHUNK, SG_CHUNK), 1)
    causal = r_i >= c_i
    w_sp = [jnp.where(causal, wsp_ref[g], 0.0).astype(jnp.bfloat16) for g in range(N_GROUPS_SG)]
    bsp = bsp_ref[...]
    sg_rows = []
    for ci in range(tm // SG_CHUNK):
        rs = slice(ci * SG_CHUNK, (ci + 1) * SG_CHUNK)
        cols = []
        for g in range(N_GROUPS_SG):
            cs = slice(g * SG_CHUNK, (g + 1) * SG_CHUNK)
            mixed = jnp.dot(w_sp[g], v[rs, cs], preferred_element_type=jnp.float32)
            cols.append(u[rs, cs] * (mixed + bsp[:, g:g + 1]))
        sg_rows.append(jnp.concatenate(cols, axis=1))
    o_sg = jnp.concatenate(sg_rows, axis=0).astype(jnp.bfloat16)

    qm = qm_ref[...]
    mem_cols = []
    for hh in range(N_HEADS_MEM):
        cs = slice(hh * HEAD_DIM, (hh + 1) * HEAD_DIM)
        kh = kv_ref[:, hh * HEAD_DIM:(hh + 1) * HEAD_DIM]
        vh = kv_ref[:, D_MEM_ATTN + hh * HEAD_DIM:D_MEM_ATTN + (hh + 1) * HEAD_DIM]
        s = lax.dot_general(qm[:, cs].astype(jnp.bfloat16), kh, (((1,), (1,)), ((), ())),
                            preferred_element_type=jnp.float32) * ATTN_SCALE
        s_max = jnp.max(s, axis=-1, keepdims=True)
        e = jnp.exp(s - s_max)
        den = jnp.sum(e, axis=-1, keepdims=True)
        o = jnp.dot(e.astype(jnp.bfloat16), vh, preferred_element_type=jnp.float32) / den
        mem_cols.append(o.astype(jnp.bfloat16))

    mix_in = jnp.concatenate([odil_ref[...].astype(jnp.bfloat16), o_sg] + mem_cols, axis=1)
    mix = jnp.dot(mix_in, wout_ref[...], preferred_element_type=jnp.float32)
    o_ref[...] = _layer_norm(DEEPNORM_ALPHA * x_ref[...] + mix, g1_ref[...], b1_ref[...])


def _mix(x2d, o_dil2d, hcat, kv, w_sp, b_sp_t, sg_g, sg_b, w_out_bf16, g1, b1, seq, tm):
    m = x2d.shape[0]
    tiles_per_batch = seq // tm
    n_mem = kv.shape[1]
    u_blk = 3 * D_DIL // D_SG
    const = lambda i: (0, 0)
    return pl.pallas_call(
        _mix_kernel,
        grid=(m // tm,),
        in_specs=[pl.BlockSpec((tm, D_MODEL), lambda i: (i, 0)),
                  pl.BlockSpec((tm, D_DIL), lambda i: (i, 0)),
                  pl.BlockSpec((tm, D_SG), lambda i: (i, u_blk)),
                  pl.BlockSpec((tm, D_SG), lambda i: (i, u_blk + 1)),
                  pl.BlockSpec((tm, D_MEM_ATTN), lambda i: (i, u_blk + 2)),
                  pl.BlockSpec((None, n_mem, 2 * D_MEM_ATTN), lambda i: (i // tiles_per_batch, 0, 0)),
                  pl.BlockSpec((N_GROUPS_SG, SG_CHUNK, SG_CHUNK), lambda i: (0, 0, 0)),
                  pl.BlockSpec((SG_CHUNK, N_GROUPS_SG), const),
                  pl.BlockSpec((1, D_SG), const),
                  pl.BlockSpec((1, D_SG), const),
                  pl.BlockSpec((D_MODEL, D_MODEL), const),
                  pl.BlockSpec((1, D_MODEL), const),
                  pl.BlockSpec((1, D_MODEL), const)],
        out_specs=pl.BlockSpec((tm, D_MODEL), lambda i: (i, 0)),
        out_shape=jax.ShapeDtypeStruct((m, D_MODEL), jnp.float32),
        compiler_params=pltpu.CompilerParams(
            dimension_semantics=("parallel",),
            vmem_limit_bytes=VMEM_LIMIT_BYTES),
        name="mix_ln1",
    )(x2d, o_dil2d, hcat, hcat, hcat, kv, w_sp, b_sp_t, sg_g, sg_b, w_out_bf16, g1, b1)


def _ffn_kernel(x_ref, wg_ref, wu_ref, wd_ref, g2_ref, b2_ref, o_ref, xb_ref):
    f = pl.program_id(1)

    @pl.when(f == 0)
    def _():
        xb_ref[...] = x_ref[...].astype(jnp.bfloat16)

    xb = xb_ref[...]
    gate = jnp.dot(xb, wg_ref[...], preferred_element_type=jnp.float32)
    up = jnp.dot(xb, wu_ref[...], preferred_element_type=jnp.float32)
    hid = (jax.nn.silu(gate) * up).astype(jnp.bfloat16)
    part = jnp.dot(hid, wd_ref[...], preferred_element_type=jnp.float32)

    @pl.when(f == 0)
    def _():
        o_ref[...] = part

    @pl.when(f > 0)
    def _():
        o_ref[...] += part

    @pl.when(f == pl.num_programs(1) - 1)
    def _():
        o_ref[...] = _layer_norm(DEEPNORM_ALPHA * x_ref[...] + o_ref[...], g2_ref[...], b2_ref[...])


def _ffn(x1, wg, wu, wd, g2, b2, tm, tf):
    m = x1.shape[0]
    d_ff = wg.shape[1]
    return pl.pallas_call(
        _ffn_kernel,
        grid=(m // tm, d_ff // tf),
        in_specs=[pl.BlockSpec((tm, D_MODEL), lambda i, f: (i, 0)),
                  pl.BlockSpec((D_MODEL, tf), lambda i, f: (0, f)),
                  pl.BlockSpec((D_MODEL, tf), lambda i, f: (0, f)),
                  pl.BlockSpec((tf, D_MODEL), lambda i, f: (f, 0)),
                  pl.BlockSpec((1, D_MODEL), lambda i, f: (0, 0)),
                  pl.BlockSpec((1, D_MODEL), lambda i, f: (0, 0))],
        out_specs=pl.BlockSpec((tm, D_MODEL), lambda i, f: (i, 0)),
        out_shape=jax.ShapeDtypeStruct((m, D_MODEL), jnp.float32),
        scratch_shapes=[pltpu.VMEM((tm, D_MODEL), jnp.bfloat16)],
        compiler_params=pltpu.CompilerParams(
            dimension_semantics=("parallel", "arbitrary"),
            vmem_limit_bytes=VMEM_LIMIT_BYTES),
        name="ffn_ln2",
    )(x1, wg, wu, wd, g2, b2)


def kernel(x, mem, w_in, rel_bias, sg_ln_g, sg_ln_b, w_spatial, b_spatial, w_mem_kv, w_out,
           ln1_g, ln1_b, w_gate, w_up, w_down, ln2_g, ln2_b):
    batch, seq, d_model = x.shape
    n_mem = mem.shape[1]
    depth = w_in.shape[0]
    bf16 = jnp.bfloat16
    h2d = x.reshape(batch * seq, d_model)
    for l in range(depth):
        hcat = _proj(h2d, w_in[l].astype(bf16), bm=1024, bn=1152, out_dtype=jnp.float32)
        kv = _proj(mem.reshape(batch * n_mem, d_model), w_mem_kv[l].astype(bf16),
                   bm=batch * n_mem, bn=2 * D_MEM_ATTN, out_dtype=bf16)
        o_dil = _dilated_attention(hcat, rel_bias, batch, seq)
        x1 = _mix(h2d, o_dil.reshape(batch * seq, D_DIL), hcat,
                  kv.reshape(batch, n_mem, 2 * D_MEM_ATTN),
                  w_spatial[l], b_spatial[l].T, sg_ln_g[l][None], sg_ln_b[l][None],
                  w_out[l].astype(bf16), ln1_g[l][None], ln1_b[l][None], seq, tm=512)
        h2d = _ffn(x1, w_gate[l].astype(bf16), w_up[l].astype(bf16), w_down[l].astype(bf16),
                   ln2_g[l][None], ln2_b[l][None], tm=512, tf=512)
    return h2d.reshape(batch, seq, d_model)
```

```python
import math

import jax
import jax.numpy as jnp
from jax import lax
from jax.experimental import pallas as pl
from jax.experimental.pallas import tpu as pltpu

D_MODEL = 2048
HEAD_DIM = 128
N_HEADS_DIL = 8
D_DIL = N_HEADS_DIL * HEAD_DIM
DILATIONS = (16, 4, 1)
N_STEPS = 128
BLOCK = 128
N_GROUPS_SG = 4
SG_CHUNK = 128
D_SG = N_GROUPS_SG * SG_CHUNK
N_HEADS_MEM = 4
D_MEM_ATTN = N_HEADS_MEM * HEAD_DIM
N_BUCKETS = 32
MAX_DISTANCE = 2048
DEEPNORM_ALPHA = 2.0 ** 0.25
LN_EPS = 1e-5
ATTN_SCALE = HEAD_DIM ** -0.5
MASK_VALUE = -1e30
N_CLASSES = 16
SUBLANES = 8
ATTN_UNROLL = 8
SG_CHUNKS_PER_STEP = 4

VMEM_LIMIT_BYTES = 58 * 1024 * 1024

_NT_DIMS = (((1,), (1,)), ((), ()))


def _layer_norm(y, g, b):
    mu = jnp.mean(y, axis=-1, keepdims=True)
    d = y - mu
    var = jnp.mean(d * d, axis=-1, keepdims=True)
    return d * lax.rsqrt(var + LN_EPS) * g + b


def _compiler_params(semantics):
    return pltpu.CompilerParams(dimension_semantics=semantics, vmem_limit_bytes=VMEM_LIMIT_BYTES)


CLASSES_PER_PROJ_TILE = 4


def _proj_grouped_kernel(*refs):
    x_refs = refs[:CLASSES_PER_PROJ_TILE]
    w_ref, o_ref, xb_ref = refs[CLASSES_PER_PROJ_TILE:]
    rows = x_refs[0].shape[0]
    for k, x_ref in enumerate(x_refs):
        xb_ref[k * rows:(k + 1) * rows, :] = x_ref[...].astype(jnp.bfloat16)
    o_ref[...] = jnp.dot(xb_ref[...], w_ref[...], preferred_element_type=jnp.float32)


def _proj_grouped(x, w_bf16, bn):
    batch, seq, d_model = x.shape
    rows = seq // N_CLASSES
    n = w_bf16.shape[1]
    tiles_per_batch = N_CLASSES // CLASSES_PER_PROJ_TILE
    bm = rows * CLASSES_PER_PROJ_TILE
    x4 = x.reshape(batch, rows, N_CLASSES * d_model)

    def x_spec(k):
        return pl.BlockSpec(
            (None, rows, d_model),
            lambda i, j: (i // tiles_per_batch, 0, CLASSES_PER_PROJ_TILE * (i % tiles_per_batch) + k))

    return pl.pallas_call(
        _proj_grouped_kernel,
        grid=(batch * tiles_per_batch, n // bn),
        in_specs=[x_spec(k) for k in range(CLASSES_PER_PROJ_TILE)]
        + [pl.BlockSpec((d_model, bn), lambda i, j: (0, j))],
        out_specs=pl.BlockSpec((bm, bn), lambda i, j: (i, j)),
        out_shape=jax.ShapeDtypeStruct((batch * seq, n), jnp.float32),
        scratch_shapes=[pltpu.VMEM((bm, d_model), jnp.bfloat16)],
        compiler_params=_compiler_params(("parallel", "parallel")),
        name="proj_grouped",
    )(*([x4] * CLASSES_PER_PROJ_TILE), w_bf16)


def _kv_kernel(x_ref, w_ref, o_ref):
    o_ref[...] = jnp.dot(x_ref[...].astype(jnp.bfloat16), w_ref[...],
                         preferred_element_type=jnp.float32).astype(o_ref.dtype)


def _kv_proj(mem2d, w_bf16):
    m, k = mem2d.shape
    n = w_bf16.shape[1]
    return pl.pallas_call(
        _kv_kernel,
        grid=(1,),
        in_specs=[pl.BlockSpec((m, k), lambda i: (0, 0)), pl.BlockSpec((k, n), lambda i: (0, 0))],
        out_specs=pl.BlockSpec((m, n), lambda i: (0, 0)),
        out_shape=jax.ShapeDtypeStruct((m, n), jnp.bfloat16),
        compiler_params=_compiler_params(("arbitrary",)),
        name="kv_proj",
    )(mem2d, w_bf16)


def _t5_bucket(dist):
    max_exact = N_BUCKETS // 2
    d = jnp.maximum(dist, 1).astype(jnp.float32)
    large = max_exact + (jnp.log(d / max_exact) / math.log(MAX_DISTANCE / max_exact)
                         * (N_BUCKETS - max_exact)).astype(jnp.int32)
    large = jnp.minimum(large, N_BUCKETS - 1)
    return jnp.where(dist < max_exact, dist, large)


def _log2(n):
    assert n & (n - 1) == 0
    return n.bit_length() - 1


def _dil_attn_kernel(rb_ref, q_ref, k_ref, v_ref, o_ref, acc_ref, m_ref, l_ref, bias_ref):
    seq = q_ref.shape[0]
    rows_per_class = seq // N_CLASSES
    h = pl.program_id(0)

    @pl.when(pl.program_id(1) == 0)
    def _():
        row = lax.broadcasted_iota(jnp.int32, (BLOCK, 2 * BLOCK), 0)
        col = lax.broadcasted_iota(jnp.int32, (BLOCK, 2 * BLOCK), 1)
        for p, dil in enumerate(DILATIONS):
            n_cls = N_CLASSES // dil
            q_rows = BLOCK // n_cls
            k_rows = 2 * BLOCK // n_cls
            q_sub = (row % q_rows) * n_cls + row // q_rows
            k_sub = (col % k_rows) * n_cls + col // k_rows
            for variant, shift in enumerate((N_STEPS, 0)):
                steps = q_sub + shift - k_sub
                valid = (steps >= 0) & (steps <= N_STEPS)
                bucket = _t5_bucket(jnp.maximum(steps, 0) * dil)
                bias = jnp.zeros((BLOCK, 2 * BLOCK), jnp.float32)
                for b in range(N_BUCKETS):
                    bias = jnp.where(bucket == b, rb_ref[b, h], bias)
                bias_ref[2 * p + variant] = jnp.where(valid, bias, MASK_VALUE)

    for p, dil in enumerate(DILATIONS):
        n_cls = N_CLASSES // dil
        q_rows = BLOCK // n_cls
        k_rows = 2 * BLOCK // n_cls
        blocks_per_sub = seq // dil // BLOCK

        def blocks_body(it, carry, p=p, dil=dil, n_cls=n_cls, q_rows=q_rows, k_rows=k_rows,
                        blocks_per_sub=blocks_per_sub):
            staged = []
            for u in range(ATTN_UNROLL):
                blk = it * ATTN_UNROLL + u
                c_sub = lax.shift_right_logical(blk, _log2(blocks_per_sub))
                n = lax.bitwise_and(blk, blocks_per_sub - 1)
                first = jnp.where(n == 0, 1, 0)
                k_back = q_rows - first * q_rows
                q_offs, k_offs = [], []
                for mcls in range(n_cls):
                    base = (c_sub + dil * mcls) * rows_per_class + n * q_rows
                    q_offs.append(pl.multiple_of(base, SUBLANES))
                    k_offs.append(pl.multiple_of(base - k_back, SUBLANES))
                qb = jnp.concatenate([q_ref[pl.ds(o, q_rows), :] for o in q_offs], axis=0)
                kb = jnp.concatenate([k_ref[pl.ds(o, k_rows), :] for o in k_offs], axis=0)
                s = lax.dot_general(qb.astype(jnp.bfloat16), kb.astype(jnp.bfloat16), _NT_DIMS,
                                    preferred_element_type=jnp.float32)
                staged.append((q_offs, k_offs, s + bias_ref[2 * p + first]))
            softmaxed = []
            for q_offs, k_offs, s in staged:
                m_blk = jnp.max(s, axis=-1, keepdims=True)
                e = jnp.exp(s - m_blk)
                l_blk = jnp.sum(e, axis=-1, keepdims=True)
                softmaxed.append((q_offs, k_offs, m_blk, l_blk, e.astype(jnp.bfloat16)))
            results = []
            for q_offs, k_offs, m_blk, l_blk, e in softmaxed:
                vb = jnp.concatenate([v_ref[pl.ds(o, k_rows), :] for o in k_offs], axis=0)
                pv = jnp.dot(e, vb.astype(jnp.bfloat16), preferred_element_type=jnp.float32)
                results.append((q_offs, m_blk, l_blk, pv))

            if p == 0:
                for q_offs, m_blk, l_blk, pv in results:
                    for mcls in range(n_cls):
                        rows = pl.ds(q_offs[mcls], q_rows)
                        sl = slice(mcls * q_rows, (mcls + 1) * q_rows)
                        m_ref[rows, :] = jnp.broadcast_to(m_blk[sl], (q_rows, HEAD_DIM))
                        l_ref[rows, :] = jnp.broadcast_to(l_blk[sl], (q_rows, HEAD_DIM))
                        acc_ref[rows, :] = pv[sl]
                return carry

            merged = []
            for q_offs, m_blk, l_blk, pv in results:
                for mcls in range(n_cls):
                    rows = pl.ds(q_offs[mcls], q_rows)
                    sl = slice(mcls * q_rows, (mcls + 1) * q_rows)
                    m_old = m_ref[rows, :]
                    m_new = jnp.maximum(m_old, m_blk[sl])
                    a_old = jnp.exp(m_old - m_new)
                    a_blk = jnp.exp(m_blk[sl] - m_new)
                    merged.append((rows, m_new,
                                   l_ref[rows, :] * a_old + l_blk[sl] * a_blk,
                                   acc_ref[rows, :] * a_old + pv[sl] * a_blk))
            for rows, m_new, l_new, acc_new in merged:
                m_ref[rows, :] = m_new
                l_ref[rows, :] = l_new
                acc_ref[rows, :] = acc_new
            return carry

        lax.fori_loop(0, dil * blocks_per_sub // ATTN_UNROLL, blocks_body, 0)

    o_ref[...] = (acc_ref[...] / l_ref[...]).astype(o_ref.dtype)


def _dilated_attention(hcat, rel_bias, batch, seq):
    d_in = hcat.shape[-1]
    hcat3 = hcat.reshape(batch, seq, d_in)
    blk = (None, seq, HEAD_DIM)
    state = pltpu.VMEM((seq, HEAD_DIM), jnp.float32)
    return pl.pallas_call(
        _dil_attn_kernel,
        grid=(N_HEADS_DIL, batch),
        in_specs=[pl.BlockSpec(memory_space=pltpu.SMEM),
                  pl.BlockSpec(blk, lambda h, b: (b, 0, h)),
                  pl.BlockSpec(blk, lambda h, b: (b, 0, N_HEADS_DIL + h)),
                  pl.BlockSpec(blk, lambda h, b: (b, 0, 2 * N_HEADS_DIL + h))],
        out_specs=pl.BlockSpec(blk, lambda h, b: (b, 0, h)),
        out_shape=jax.ShapeDtypeStruct((batch, seq, D_DIL), jnp.bfloat16),
        scratch_shapes=[state, state, state,
                        pltpu.VMEM((2 * len(DILATIONS), BLOCK, 2 * BLOCK), jnp.float32)],
        compiler_params=_compiler_params(("arbitrary", "arbitrary")),
        name="dilated_attn",
    )(rel_bias, hcat3, hcat3, hcat3)


def _sg_kernel(u_ref, v_ref, wsp_ref, bsp_ref, g_ref, b_ref, o_ref):
    rows_per_chunk = SG_CHUNK // N_CLASSES
    r_i = lax.broadcasted_iota(jnp.int32, (SG_CHUNK, SG_CHUNK), 0)
    c_i = lax.broadcasted_iota(jnp.int32, (SG_CHUNK, SG_CHUNK), 1)
    pos_r = (r_i % rows_per_chunk) * N_CLASSES + r_i // rows_per_chunk
    pos_c = (c_i % rows_per_chunk) * N_CLASSES + c_i // rows_per_chunk
    causal = pos_r >= pos_c
    w_sp = [jnp.where(causal, wsp_ref[g], 0.0).astype(jnp.bfloat16) for g in range(N_GROUPS_SG)]
    bsp = bsp_ref[...]
    for kk in range(SG_CHUNKS_PER_STEP):
        rs = slice(kk * rows_per_chunk, (kk + 1) * rows_per_chunk)
        u = jax.nn.gelu(u_ref[:, rs, :].reshape(SG_CHUNK, D_SG))
        v = jax.nn.gelu(v_ref[:, rs, :].reshape(SG_CHUNK, D_SG))
        v = _layer_norm(v, g_ref[...], b_ref[...]).astype(jnp.bfloat16)
        cols = []
        for g in range(N_GROUPS_SG):
            cs = slice(g * SG_CHUNK, (g + 1) * SG_CHUNK)
            mixed = jnp.dot(w_sp[g], v[:, cs], preferred_element_type=jnp.float32)
            cols.append(u[:, cs] * (mixed + bsp[:, g:g + 1]))
        o_ref[:, rs, :] = jnp.concatenate(cols, axis=1).reshape(N_CLASSES, rows_per_chunk, D_SG)


def _spatial_gating(hcat, w_sp_grouped, b_sp_grouped_t, sg_g, sg_b, batch, seq):
    d_in = hcat.shape[-1]
    rows_per_class = seq // N_CLASSES
    step_rows = SG_CHUNKS_PER_STEP * SG_CHUNK // N_CLASSES
    steps = rows_per_class // step_rows
    hcat5 = hcat.reshape(batch, N_CLASSES, steps, step_rows, d_in)
    u_blk = 3 * D_DIL // D_SG
    blk = (None, N_CLASSES, None, step_rows, D_SG)
    const2 = lambda b, k: (0, 0)
    out = pl.pallas_call(
        _sg_kernel,
        grid=(batch, steps),
        in_specs=[pl.BlockSpec(blk, lambda b, k: (b, 0, k, 0, u_blk)),
                  pl.BlockSpec(blk, lambda b, k: (b, 0, k, 0, u_blk + 1)),
                  pl.BlockSpec((N_GROUPS_SG, SG_CHUNK, SG_CHUNK), lambda b, k: (0, 0, 0)),
                  pl.BlockSpec((SG_CHUNK, N_GROUPS_SG), const2),
                  pl.BlockSpec((1, D_SG), const2),
                  pl.BlockSpec((1, D_SG), const2)],
        out_specs=pl.BlockSpec(blk, lambda b, k: (b, 0, k, 0, 0)),
        out_shape=jax.ShapeDtypeStruct((batch, N_CLASSES, steps, step_rows, D_SG), jnp.float32),
        compiler_params=_compiler_params(("parallel", "parallel")),
        name="spatial_gating",
    )(hcat5, hcat5, w_sp_grouped, b_sp_grouped_t, sg_g, sg_b)
    return out.reshape(batch * seq, D_SG)


def _mix_kernel(x_ref, odil_ref, osg_ref, qm_ref, kv_ref, wout_ref, g1_ref, b1_ref, o_ref, ob_ref):
    qm = qm_ref[...]
    mem_cols = []
    for hh in range(N_HEADS_MEM):
        cs = slice(hh * HEAD_DIM, (hh + 1) * HEAD_DIM)
        kh = kv_ref[:, hh * HEAD_DIM:(hh + 1) * HEAD_DIM]
        vh = kv_ref[:, D_MEM_ATTN + hh * HEAD_DIM:D_MEM_ATTN + (hh + 1) * HEAD_DIM]
        s = lax.dot_general(qm[:, cs].astype(jnp.bfloat16), kh, _NT_DIMS,
                            preferred_element_type=jnp.float32) * ATTN_SCALE
        s_max = jnp.max(s, axis=-1, keepdims=True)
        e = jnp.exp(s - s_max)
        den = jnp.sum(e, axis=-1, keepdims=True)
        o = jnp.dot(e.astype(jnp.bfloat16), vh, preferred_element_type=jnp.float32) / den
        mem_cols.append(o.astype(jnp.bfloat16))

    mix_in = jnp.concatenate([odil_ref[...], osg_ref[...].astype(jnp.bfloat16)] + mem_cols, axis=1)
    mix = jnp.dot(mix_in, wout_ref[...], preferred_element_type=jnp.float32)
    x1 = _layer_norm(DEEPNORM_ALPHA * x_ref[...] + mix, g1_ref[...], b1_ref[...])
    o_ref[...] = x1
    ob_ref[...] = x1.astype(jnp.bfloat16)


def _mix(x, o_dil2d, o_sg2d, hcat, kv, w_out_bf16, g1, b1):
    batch, seq, d_model = x.shape
    rows = seq // N_CLASSES
    n_mem = kv.shape[1]
    qm_blk = (3 * D_DIL + 2 * D_SG) // D_MEM_ATTN
    x4 = x.reshape(batch, rows, N_CLASSES * d_model)
    x_blk = (None, rows, d_model)
    x_map = lambda i: (i // N_CLASSES, 0, i % N_CLASSES)
    const = lambda i: (0, 0)
    x1, x1b = pl.pallas_call(
        _mix_kernel,
        grid=(batch * N_CLASSES,),
        in_specs=[pl.BlockSpec(x_blk, x_map),
                  pl.BlockSpec((rows, D_DIL), lambda i: (i, 0)),
                  pl.BlockSpec((rows, D_SG), lambda i: (i, 0)),
                  pl.BlockSpec((rows, D_MEM_ATTN), lambda i: (i, qm_blk)),
                  pl.BlockSpec((None, n_mem, 2 * D_MEM_ATTN), lambda i: (i // N_CLASSES, 0, 0)),
                  pl.BlockSpec((d_model, d_model), const),
                  pl.BlockSpec((1, d_model), const),
                  pl.BlockSpec((1, d_model), const)],
        out_specs=[pl.BlockSpec(x_blk, x_map), pl.BlockSpec(x_blk, x_map)],
        out_shape=[jax.ShapeDtypeStruct(x4.shape, jnp.float32),
                   jax.ShapeDtypeStruct(x4.shape, jnp.bfloat16)],
        compiler_params=_compiler_params(("parallel",)),
        name="mix_ln1",
    )(x4, o_dil2d, o_sg2d, hcat, kv, w_out_bf16, g1, b1)
    return x1.reshape(batch * seq, d_model), x1b.reshape(batch * seq, d_model)


FFN_COL_SPLIT = 2


def _ffn_hidden_kernel(x_ref, wg_ref, wu_ref, h_ref):
    x = x_ref[...]
    bn = h_ref.shape[1]
    part = bn // FFN_COL_SPLIT
    for k in range(FFN_COL_SPLIT):
        cs = slice(k * part, (k + 1) * part)
        gate = jnp.dot(x, wg_ref[:, cs], preferred_element_type=jnp.float32)
        up = jnp.dot(x, wu_ref[:, cs], preferred_element_type=jnp.float32)
        h_ref[:, cs] = (jax.nn.silu(gate) * up).astype(h_ref.dtype)


def _ffn_hidden(x1b, wg, wu, bm, bn):
    m, d_model = x1b.shape
    d_ff = wg.shape[1]
    return pl.pallas_call(
        _ffn_hidden_kernel,
        grid=(m // bm, d_ff // bn),
        in_specs=[pl.BlockSpec((bm, d_model), lambda i, j: (i, 0)),
                  pl.BlockSpec((d_model, bn), lambda i, j: (0, j)),
                  pl.BlockSpec((d_model, bn), lambda i, j: (0, j))],
        out_specs=pl.BlockSpec((bm, bn), lambda i, j: (i, j)),
        out_shape=jax.ShapeDtypeStruct((m, d_ff), jnp.bfloat16),
        compiler_params=_compiler_params(("parallel", "parallel")),
        name="ffn_hidden",
    )(x1b, wg, wu)


def _ffn_down_kernel(h_ref, x_ref, wd_ref, g2_ref, b2_ref, o_ref):
    f = jnp.dot(h_ref[...], wd_ref[...], preferred_element_type=jnp.float32)
    o_ref[...] = _layer_norm(DEEPNORM_ALPHA * x_ref[...] + f, g2_ref[...], b2_ref[...])


def _ffn_down(hid, x1, wd, g2, b2, tm):
    m, d_ff = hid.shape
    d_model = wd.shape[1]
    const = lambda i: (0, 0)
    return pl.pallas_call(
        _ffn_down_kernel,
        grid=(m // tm,),
        in_specs=[pl.BlockSpec((tm, d_ff), lambda i: (i, 0)),
                  pl.BlockSpec((tm, d_model), lambda i: (i, 0)),
                  pl.BlockSpec((d_ff, d_model), const, pipeline_mode=pl.Buffered(1)),
                  pl.BlockSpec((1, d_model), const),
                  pl.BlockSpec((1, d_model), const)],
        out_specs=pl.BlockSpec((tm, d_model), lambda i: (i, 0)),
        out_shape=jax.ShapeDtypeStruct((m, d_model), jnp.float32),
        compiler_params=_compiler_params(("parallel",)),
        name="ffn_down_ln2",
    )(hid, x1, wd, g2, b2)


def _group_gate_weights(w_spatial, b_spatial):
    g, c, _ = w_spatial.shape
    r = c // N_CLASSES
    w = w_spatial.reshape(g, r, N_CLASSES, r, N_CLASSES).transpose(0, 2, 1, 4, 3).reshape(g, c, c)
    b = b_spatial.reshape(g, r, N_CLASSES).transpose(0, 2, 1).reshape(g, c)
    return w, b.T


def kernel(x, mem, w_in, rel_bias, sg_ln_g, sg_ln_b, w_spatial, b_spatial, w_mem_kv, w_out,
           ln1_g, ln1_b, w_gate, w_up, w_down, ln2_g, ln2_b):
    batch, seq, d_model = x.shape
    n_mem = mem.shape[1]
    depth = w_in.shape[0]
    bf16 = jnp.bfloat16
    q_scale = jnp.concatenate([jnp.full((D_DIL,), ATTN_SCALE, jnp.float32),
                               jnp.ones((w_in.shape[2] - D_DIL,), jnp.float32)])
    for l in range(depth):
        hcat = _proj_grouped(x, (w_in[l] * q_scale).astype(bf16), bn=1152)
        kv = _kv_proj(mem.reshape(batch * n_mem, d_model), w_mem_kv[l].astype(bf16))
        o_dil = _dilated_attention(hcat, rel_bias, batch, seq)
        w_sp, b_sp_t = _group_gate_weights(w_spatial[l], b_spatial[l])
        o_sg = _spatial_gating(hcat, w_sp, b_sp_t, sg_ln_g[l][None], sg_ln_b[l][None], batch, seq)
        x1, x1b = _mix(x, o_dil.reshape(batch * seq, D_DIL), o_sg, hcat,
                       kv.reshape(batch, n_mem, 2 * D_MEM_ATTN),
                       w_out[l].astype(bf16), ln1_g[l][None], ln1_b[l][None])
        hid = _ffn_hidden(x1b, w_gate[l].astype(bf16), w_up[l].astype(bf16), bm=1024, bn=512)
        x = _ffn_down(hid, x1, w_down[l].astype(bf16), ln2_g[l][None], ln2_b[l][None],
                      tm=512).reshape(batch, seq, d_model)
    return x
```

```python
import math

import jax
import jax.numpy as jnp
from jax import lax
from jax.experimental import pallas as pl
from jax.experimental.pallas import tpu as pltpu

D_MODEL = 2048
HEAD_DIM = 128
N_HEADS_DIL = 8
D_DIL = N_HEADS_DIL * HEAD_DIM
DILATIONS = (16, 4, 1)
N_STEPS = 128
BLOCK = 128
N_GROUPS_SG = 4
SG_CHUNK = 128
D_SG = N_GROUPS_SG * SG_CHUNK
N_HEADS_MEM = 4
D_MEM_ATTN = N_HEADS_MEM * HEAD_DIM
N_BUCKETS = 32
MAX_DISTANCE = 2048
DEEPNORM_ALPHA = 2.0 ** 0.25
LN_EPS = 1e-5
ATTN_SCALE = HEAD_DIM ** -0.5
MASK_VALUE = -1e30
N_CLASSES = 16
REGROUP_STRIDE = 4
SUBLANES = 8
ATTN_UNROLL = 8

VMEM_LIMIT_BYTES = 58 * 1024 * 1024

_NT_DIMS = (((1,), (1,)), ((), ()))


def _layer_norm(y, g, b):
    mu = jnp.mean(y, axis=-1, keepdims=True)
    d = y - mu
    var = jnp.mean(d * d, axis=-1, keepdims=True)
    return d * lax.rsqrt(var + LN_EPS) * g + b


def _compiler_params(semantics):
    return pltpu.CompilerParams(dimension_semantics=semantics, vmem_limit_bytes=VMEM_LIMIT_BYTES)


def _resident(block_shape, index_map):
    return pl.BlockSpec(block_shape, index_map, pipeline_mode=pl.Buffered(1))


def _proj_kernel(x_ref, w_ref, o_ref, xb_ref):
    @pl.when(pl.program_id(1) == 0)
    def _():
        xb_ref[...] = x_ref[...].astype(jnp.bfloat16)

    o_ref[...] = jnp.dot(xb_ref[...], w_ref[...],
                         preferred_element_type=jnp.float32).astype(o_ref.dtype)


def _proj(x2d, w_bf16, bm, bn, out_dtype):
    m, k = x2d.shape
    n = w_bf16.shape[1]
    return pl.pallas_call(
        _proj_kernel,
        grid=(m // bm, n // bn),
        in_specs=[pl.BlockSpec((bm, k), lambda i, j: (i, 0)),
                  pl.BlockSpec((k, bn), lambda i, j: (0, j))],
        out_specs=pl.BlockSpec((bm, bn), lambda i, j: (i, j)),
        out_shape=jax.ShapeDtypeStruct((m, n), out_dtype),
        scratch_shapes=[pltpu.VMEM((bm, k), jnp.bfloat16)],
        compiler_params=_compiler_params(("parallel", "arbitrary")),
        name="proj",
    )(x2d, w_bf16)


def _t5_bucket(dist):
    max_exact = N_BUCKETS // 2
    d = jnp.maximum(dist, 1).astype(jnp.float32)
    large = max_exact + (jnp.log(d / max_exact) / math.log(MAX_DISTANCE / max_exact)
                         * (N_BUCKETS - max_exact)).astype(jnp.int32)
    large = jnp.minimum(large, N_BUCKETS - 1)
    return jnp.where(dist < max_exact, dist, large)


def _log2(n):
    assert n & (n - 1) == 0
    return n.bit_length() - 1


def _class_slot(c_sub, dil, mcls):
    if dil == 1:
        return REGROUP_STRIDE * (mcls % REGROUP_STRIDE) + mcls // REGROUP_STRIDE
    if dil == REGROUP_STRIDE:
        return c_sub * REGROUP_STRIDE + mcls
    assert dil == N_CLASSES and mcls == 0
    lo = lax.bitwise_and(c_sub, REGROUP_STRIDE - 1)
    hi = lax.shift_right_logical(c_sub, _log2(REGROUP_STRIDE))
    return lo * REGROUP_STRIDE + hi


def _regroup(src_ref, tmp_ref, dst_ref):
    seq = src_ref.shape[0]
    quarter = seq // REGROUP_STRIDE
    rows_per_class = seq // N_CLASSES
    for lo in range(REGROUP_STRIDE):
        tmp_ref[lo * quarter:(lo + 1) * quarter, :] = src_ref[pl.ds(lo, quarter, stride=REGROUP_STRIDE), :]
    for lo in range(REGROUP_STRIDE):
        for hi in range(REGROUP_STRIDE):
            slot = lo * REGROUP_STRIDE + hi
            dst_ref[slot * rows_per_class:(slot + 1) * rows_per_class, :] = (
                tmp_ref[pl.ds(lo * quarter + hi, rows_per_class, stride=REGROUP_STRIDE), :])


def _ungroup(read_rows, tmp_ref, dst_ref):
    seq = dst_ref.shape[0]
    quarter = seq // REGROUP_STRIDE
    rows_per_class = seq // N_CLASSES
    for lo in range(REGROUP_STRIDE):
        for hi in range(REGROUP_STRIDE):
            slot = lo * REGROUP_STRIDE + hi
            tmp_ref[pl.ds(lo * quarter + hi, rows_per_class, stride=REGROUP_STRIDE), :] = (
                read_rows(slot * rows_per_class, rows_per_class))
    for lo in range(REGROUP_STRIDE):
        dst_ref[pl.ds(lo, quarter, stride=REGROUP_STRIDE), :] = tmp_ref[lo * quarter:(lo + 1) * quarter, :]


def _dil_attn_kernel(rb_ref, q_ref, k_ref, v_ref, o_ref,
                     qs_ref, ks_ref, vs_ref, tmp_ref, acc_ref, m_ref, l_ref, bias_ref):
    seq = q_ref.shape[0]
    rows_per_class = seq // N_CLASSES
    h = pl.program_id(0)

    _regroup(q_ref, tmp_ref, qs_ref)
    _regroup(k_ref, tmp_ref, ks_ref)
    _regroup(v_ref, tmp_ref, vs_ref)

    @pl.when(pl.program_id(1) == 0)
    def _():
        row = lax.broadcasted_iota(jnp.int32, (BLOCK, 2 * BLOCK), 0)
        col = lax.broadcasted_iota(jnp.int32, (BLOCK, 2 * BLOCK), 1)
        for p, dil in enumerate(DILATIONS):
            n_cls = N_CLASSES // dil
            q_rows = BLOCK // n_cls
            k_rows = 2 * BLOCK // n_cls
            q_sub = (row % q_rows) * n_cls + row // q_rows
            k_sub = (col % k_rows) * n_cls + col // k_rows
            for variant, shift in enumerate((N_STEPS, 0)):
                steps = q_sub + shift - k_sub
                valid = (steps >= 0) & (steps <= N_STEPS)
                bucket = _t5_bucket(jnp.maximum(steps, 0) * dil)
                bias = jnp.zeros((BLOCK, 2 * BLOCK), jnp.float32)
                for b in range(N_BUCKETS):
                    bias = jnp.where(bucket == b, rb_ref[b, h], bias)
                bias_ref[2 * p + variant] = jnp.where(valid, bias, MASK_VALUE)

    for p, dil in enumerate(DILATIONS):
        n_cls = N_CLASSES // dil
        q_rows = BLOCK // n_cls
        k_rows = 2 * BLOCK // n_cls
        blocks_per_sub = seq // dil // BLOCK

        def blocks_body(it, carry, p=p, dil=dil, n_cls=n_cls, q_rows=q_rows, k_rows=k_rows,
                        blocks_per_sub=blocks_per_sub):
            staged = []
            for u in range(ATTN_UNROLL):
                blk = it * ATTN_UNROLL + u
                c_sub = lax.shift_right_logical(blk, _log2(blocks_per_sub))
                n = lax.bitwise_and(blk, blocks_per_sub - 1)
                first = jnp.where(n == 0, 1, 0)
                k_back = q_rows - first * q_rows
                q_offs, k_offs = [], []
                for mcls in range(n_cls):
                    base = _class_slot(c_sub, dil, mcls) * rows_per_class + n * q_rows
                    q_offs.append(pl.multiple_of(base, SUBLANES))
                    k_offs.append(pl.multiple_of(base - k_back, SUBLANES))
                qb = jnp.concatenate([qs_ref[pl.ds(o, q_rows), :] for o in q_offs], axis=0)
                kb = jnp.concatenate([ks_ref[pl.ds(o, k_rows), :] for o in k_offs], axis=0)
                s = lax.dot_general(qb.astype(jnp.bfloat16), kb.astype(jnp.bfloat16), _NT_DIMS,
                                    preferred_element_type=jnp.float32)
                staged.append((q_offs, k_offs, s + bias_ref[2 * p + first]))
            softmaxed = []
            for q_offs, k_offs, s in staged:
                m_blk = jnp.max(s, axis=-1, keepdims=True)
                e = jnp.exp(s - m_blk)
                l_blk = jnp.sum(e, axis=-1, keepdims=True)
                softmaxed.append((q_offs, k_offs, m_blk, l_blk, e.astype(jnp.bfloat16)))
            results = []
            for q_offs, k_offs, m_blk, l_blk, e in softmaxed:
                vb = jnp.concatenate([vs_ref[pl.ds(o, k_rows), :] for o in k_offs], axis=0)
                pv = jnp.dot(e, vb.astype(jnp.bfloat16), preferred_element_type=jnp.float32)
                results.append((q_offs, m_blk, l_blk, pv))

            if p == 0:
                for q_offs, m_blk, l_blk, pv in results:
                    for mcls in range(n_cls):
                        rows = pl.ds(q_offs[mcls], q_rows)
                        sl = slice(mcls * q_rows, (mcls + 1) * q_rows)
                        m_ref[rows, :] = jnp.broadcast_to(m_blk[sl], (q_rows, HEAD_DIM))
                        l_ref[rows, :] = jnp.broadcast_to(l_blk[sl], (q_rows, HEAD_DIM))
                        acc_ref[rows, :] = pv[sl]
                return carry

            merged = []
            for q_offs, m_blk, l_blk, pv in results:
                for mcls in range(n_cls):
                    rows = pl.ds(q_offs[mcls], q_rows)
                    sl = slice(mcls * q_rows, (mcls + 1) * q_rows)
                    m_old = m_ref[rows, :]
                    m_new = jnp.maximum(m_old, m_blk[sl])
                    a_old = jnp.exp(m_old - m_new)
                    a_blk = jnp.exp(m_blk[sl] - m_new)
                    merged.append((rows, m_new,
                                   l_ref[rows, :] * a_old + l_blk[sl] * a_blk,
                                   acc_ref[rows, :] * a_old + pv[sl] * a_blk))
            for rows, m_new, l_new, acc_new in merged:
                m_ref[rows, :] = m_new
                l_ref[rows, :] = l_new
                acc_ref[rows, :] = acc_new
            return carry

        lax.fori_loop(0, dil * blocks_per_sub // ATTN_UNROLL, blocks_body, 0)

    def normalised(start, size):
        return acc_ref[start:start + size, :] / l_ref[start:start + size, :]

    _ungroup(normalised, tmp_ref, o_ref)


def _dilated_attention(hcat, rel_bias, batch, seq):
    hcat3 = hcat.reshape(batch, seq, hcat.shape[-1])
    blk = (None, seq, HEAD_DIM)
    rows = pltpu.VMEM((seq, HEAD_DIM), jnp.float32)
    return pl.pallas_call(
        _dil_attn_kernel,
        grid=(N_HEADS_DIL, batch),
        in_specs=[pl.BlockSpec(memory_space=pltpu.SMEM),
                  pl.BlockSpec(blk, lambda h, b: (b, 0, h)),
                  pl.BlockSpec(blk, lambda h, b: (b, 0, N_HEADS_DIL + h)),
                  pl.BlockSpec(blk, lambda h, b: (b, 0, 2 * N_HEADS_DIL + h))],
        out_specs=pl.BlockSpec(blk, lambda h, b: (b, 0, h)),
        out_shape=jax.ShapeDtypeStruct((batch, seq, D_DIL), jnp.float32),
        scratch_shapes=[rows] * 7
        + [pltpu.VMEM((2 * len(DILATIONS), BLOCK, 2 * BLOCK), jnp.float32)],
        compiler_params=_compiler_params(("arbitrary", "arbitrary")),
        name="dilated_attn",
    )(rel_bias, hcat3, hcat3, hcat3)


def _mix_kernel(x_ref, odil_ref, u_ref, v_ref, qm_ref, kv_ref, wsp_ref, bsp_ref, sgg_ref, sgb_ref,
                wout_ref, g1_ref, b1_ref, o_ref, ob_ref):
    tm = x_ref.shape[0]

    u = jax.nn.gelu(u_ref[...])
    v = _layer_norm(jax.nn.gelu(v_ref[...]), sgg_ref[...], sgb_ref[...]).astype(jnp.bfloat16)
    r_i = lax.broadcasted_iota(jnp.int32, (SG_CHUNK, SG_CHUNK), 0)
    c_i = lax.broadcasted_iota(jnp.int32, (SG_CHUNK, SG_CHUNK), 1)
    causal = r_i >= c_i
    w_sp = [jnp.where(causal, wsp_ref[g], 0.0).astype(jnp.bfloat16) for g in range(N_GROUPS_SG)]
    bsp = bsp_ref[...]
    sg_rows = []
    for ci in range(tm // SG_CHUNK):
        rs = slice(ci * SG_CHUNK, (ci + 1) * SG_CHUNK)
        cols = []
        for g in range(N_GROUPS_SG):
            cs = slice(g * SG_CHUNK, (g + 1) * SG_CHUNK)
            mixed = jnp.dot(w_sp[g], v[rs, cs], preferred_element_type=jnp.float32)
            cols.append((u[rs, cs] * (mixed + bsp[:, g:g + 1])).astype(jnp.bfloat16))
        sg_rows.append(jnp.concatenate(cols, axis=1))
    o_sg = jnp.concatenate(sg_rows, axis=0)

    qm = qm_ref[...].astype(jnp.bfloat16)
    mem_cols = []
    for hh in range(N_HEADS_MEM):
        cs = slice(hh * HEAD_DIM, (hh + 1) * HEAD_DIM)
        kh = kv_ref[:, hh * HEAD_DIM:(hh + 1) * HEAD_DIM]
        vh = kv_ref[:, D_MEM_ATTN + hh * HEAD_DIM:D_MEM_ATTN + (hh + 1) * HEAD_DIM]
        s = lax.dot_general(qm[:, cs], kh, _NT_DIMS,
                            preferred_element_type=jnp.float32) * ATTN_SCALE
        s_max = jnp.max(s, axis=-1, keepdims=True)
        e = jnp.exp(s - s_max)
        den = jnp.sum(e, axis=-1, keepdims=True)
        o = jnp.dot(e.astype(jnp.bfloat16), vh, preferred_element_type=jnp.float32) / den
        mem_cols.append(o.astype(jnp.bfloat16))

    mix_in = jnp.concatenate([odil_ref[...].astype(jnp.bfloat16), o_sg] + mem_cols, axis=1)
    mix = jnp.dot(mix_in, wout_ref[...], preferred_element_type=jnp.float32)
    x1 = _layer_norm(DEEPNORM_ALPHA * x_ref[...] + mix, g1_ref[...], b1_ref[...])
    o_ref[...] = x1
    ob_ref[...] = x1.astype(jnp.bfloat16)


def _mix(x2d, o_dil2d, hcat, kv, w_sp, b_sp_t, sg_g, sg_b, w_out, g1, b1, seq, tm):
    m, d_model = x2d.shape
    tiles_per_batch = seq // tm
    n_mem = kv.shape[1]
    u_blk = 3 * D_DIL // D_SG
    const = lambda i: (0, 0)
    return pl.pallas_call(
        _mix_kernel,
        grid=(m // tm,),
        in_specs=[pl.BlockSpec((tm, d_model), lambda i: (i, 0)),
                  pl.BlockSpec((tm, D_DIL), lambda i: (i, 0)),
                  pl.BlockSpec((tm, D_SG), lambda i: (i, u_blk)),
                  pl.BlockSpec((tm, D_SG), lambda i: (i, u_blk + 1)),
                  pl.BlockSpec((tm, D_MEM_ATTN), lambda i: (i, u_blk + 2)),
                  pl.BlockSpec((None, n_mem, 2 * D_MEM_ATTN), lambda i: (i // tiles_per_batch, 0, 0)),
                  _resident((N_GROUPS_SG, SG_CHUNK, SG_CHUNK), lambda i: (0, 0, 0)),
                  _resident((SG_CHUNK, N_GROUPS_SG), const),
                  _resident((1, D_SG), const),
                  _resident((1, D_SG), const),
                  _resident((d_model, d_model), const),
                  _resident((1, d_model), const),
                  _resident((1, d_model), const)],
        out_specs=[pl.BlockSpec((tm, d_model), lambda i: (i, 0)),
                   pl.BlockSpec((tm, d_model), lambda i: (i, 0))],
        out_shape=[jax.ShapeDtypeStruct((m, d_model), jnp.float32),
                   jax.ShapeDtypeStruct((m, d_model), jnp.bfloat16)],
        compiler_params=_compiler_params(("parallel",)),
        name="mix_ln1",
    )(x2d, o_dil2d, hcat, hcat, hcat, kv, w_sp, b_sp_t, sg_g, sg_b, w_out, g1, b1)


FFN_COL_SPLIT = 2


def _ffn_hidden_kernel(x_ref, wg_ref, wu_ref, h_ref):
    x = x_ref[...]
    bn = h_ref.shape[1]
    part = bn // FFN_COL_SPLIT
    for k in range(FFN_COL_SPLIT):
        cs = slice(k * part, (k + 1) * part)
        gate = jnp.dot(x, wg_ref[:, cs], preferred_element_type=jnp.float32)
        up = jnp.dot(x, wu_ref[:, cs], preferred_element_type=jnp.float32)
        h_ref[:, cs] = (jax.nn.silu(gate) * up).astype(h_ref.dtype)


def _ffn_hidden(x1b, wg, wu, bm, bn):
    m, d_model = x1b.shape
    d_ff = wg.shape[1]
    return pl.pallas_call(
        _ffn_hidden_kernel,
        grid=(m // bm, d_ff // bn),
        in_specs=[pl.BlockSpec((bm, d_model), lambda i, j: (i, 0)),
                  pl.BlockSpec((d_model, bn), lambda i, j: (0, j)),
                  pl.BlockSpec((d_model, bn), lambda i, j: (0, j))],
        out_specs=pl.BlockSpec((bm, bn), lambda i, j: (i, j)),
        out_shape=jax.ShapeDtypeStruct((m, d_ff), jnp.bfloat16),
        compiler_params=_compiler_params(("parallel", "parallel")),
        name="ffn_hidden",
    )(x1b, wg, wu)


def _ffn_down_kernel(h_ref, x_ref, wd_ref, g2_ref, b2_ref, o_ref):
    f = jnp.dot(h_ref[...], wd_ref[...], preferred_element_type=jnp.float32)
    o_ref[...] = _layer_norm(DEEPNORM_ALPHA * x_ref[...] + f, g2_ref[...], b2_ref[...])


def _ffn_down(hid, x1, wd, g2, b2, tm):
    m, d_ff = hid.shape
    d_model = wd.shape[1]
    const = lambda i: (0, 0)
    return pl.pallas_call(
        _ffn_down_kernel,
        grid=(m // tm,),
        in_specs=[pl.BlockSpec((tm, d_ff), lambda i: (i, 0)),
                  pl.BlockSpec((tm, d_model), lambda i: (i, 0)),
                  _resident((d_ff, d_model), const),
                  _resident((1, d_model), const),
                  _resident((1, d_model), const)],
        out_specs=pl.BlockSpec((tm, d_model), lambda i: (i, 0)),
        out_shape=jax.ShapeDtypeStruct((m, d_model), jnp.float32),
        compiler_params=_compiler_params(("parallel",)),
        name="ffn_down_ln2",
    )(hid, x1, wd, g2, b2)


def kernel(x, mem, w_in, rel_bias, sg_ln_g, sg_ln_b, w_spatial, b_spatial, w_mem_kv, w_out,
           ln1_g, ln1_b, w_gate, w_up, w_down, ln2_g, ln2_b):
    batch, seq, d_model = x.shape
    n_mem = mem.shape[1]
    depth = w_in.shape[0]
    bf16 = jnp.bfloat16
    q_scale = jnp.concatenate([jnp.full((D_DIL,), ATTN_SCALE, jnp.float32),
                               jnp.ones((w_in.shape[2] - D_DIL,), jnp.float32)])
    h2d = x.reshape(batch * seq, d_model)
    for l in range(depth):
        hcat = _proj(h2d, (w_in[l] * q_scale).astype(bf16), bm=1024, bn=1152, out_dtype=jnp.float32)
        kv = _proj(mem.reshape(batch * n_mem, d_model), w_mem_kv[l].astype(bf16),
                   bm=batch * n_mem, bn=2 * D_MEM_ATTN, out_dtype=bf16)
        o_dil = _dilated_attention(hcat, rel_bias, batch, seq)
        x1, x1b = _mix(h2d, o_dil.reshape(batch * seq, D_DIL), hcat,
                       kv.reshape(batch, n_mem, 2 * D_MEM_ATTN), w_spatial[l], b_spatial[l].T,
                       sg_ln_g[l][None], sg_ln_b[l][None], w_out[l].astype(bf16),
                       ln1_g[l][None], ln1_b[l][None], seq, tm=512)
        hid = _ffn_hidden(x1b, w_gate[l].astype(bf16), w_up[l].astype(bf16), bm=1024, bn=512)
        h2d = _ffn_down(hid, x1, w_down[l].astype(bf16), ln2_g[l][None], ln2_b[l][None], tm=512)
    return h2d.reshape(batch, seq, d_model)
```

```python
import math

import jax
import jax.numpy as jnp
from jax import lax
from jax.experimental import pallas as pl
from jax.experimental.pallas import tpu as pltpu

D_MODEL = 2048
HEAD_DIM = 128
N_HEADS_DIL = 8
D_DIL = N_HEADS_DIL * HEAD_DIM
DILATIONS = (16, 4, 1)
N_STEPS = 128
BLOCK = 128
N_GROUPS_SG = 4
SG_CHUNK = 128
D_SG = N_GROUPS_SG * SG_CHUNK
N_HEADS_MEM = 4
D_MEM_ATTN = N_HEADS_MEM * HEAD_DIM
N_BUCKETS = 32
MAX_DISTANCE = 2048
DEEPNORM_ALPHA = 2.0 ** 0.25
LN_EPS = 1e-5
ATTN_SCALE = HEAD_DIM ** -0.5
LOG2_E = math.log2(math.e)
MASK_VALUE = -1e30
N_CLASSES = 16
REGROUP_STRIDE = 4
SUBLANES = 8
ATTN_UNROLL = 8
COMBINE_ROWS = 64
ROW_SPLIT = 2

VMEM_LIMIT_BYTES = 58 * 1024 * 1024

_NT_DIMS = (((1,), (1,)), ((), ()))


def _layer_norm(y, g, b):
    mu = jnp.mean(y, axis=-1, keepdims=True)
    d = y - mu
    var = jnp.mean(d * d, axis=-1, keepdims=True)
    return d * lax.rsqrt(var + LN_EPS) * g + b


def _compiler_params(semantics):
    return pltpu.CompilerParams(dimension_semantics=semantics, vmem_limit_bytes=VMEM_LIMIT_BYTES)


def _resident(block_shape, index_map):
    return pl.BlockSpec(block_shape, index_map, pipeline_mode=pl.Buffered(1))


def _proj_kernel(x_ref, w_ref, side_ref, o_ref, side_out_ref, xb_ref):
    @pl.when(pl.program_id(1) == 0)
    def _():
        xb_ref[...] = x_ref[...].astype(jnp.bfloat16)

    side_out_ref[...] = side_ref[...].astype(jnp.bfloat16)
    o_ref[...] = jnp.dot(xb_ref[...], w_ref[...], preferred_element_type=jnp.float32)


def _proj(x2d, w_bf16, w_side, bm, bn):
    m, k = x2d.shape
    n = w_bf16.shape[1]
    grid = (m // bm, n // bn)
    side_rows = w_side.shape[0] // (grid[0] * grid[1])
    side_spec = pl.BlockSpec((side_rows, w_side.shape[1]), lambda i, j: (i * grid[1] + j, 0))
    return pl.pallas_call(
        _proj_kernel,
        grid=grid,
        in_specs=[pl.BlockSpec((bm, k), lambda i, j: (i, 0)),
                  pl.BlockSpec((k, bn), lambda i, j: (0, j)),
                  side_spec],
        out_specs=[pl.BlockSpec((bm, bn), lambda i, j: (i, j)), side_spec],
        out_shape=[jax.ShapeDtypeStruct((m, n), jnp.float32),
                   jax.ShapeDtypeStruct(w_side.shape, jnp.bfloat16)],
        scratch_shapes=[pltpu.VMEM((bm, k), jnp.bfloat16)],
        compiler_params=_compiler_params(("parallel", "arbitrary")),
        name="proj",
    )(x2d, w_bf16, w_side)


def _kv_kernel(x_ref, w_ref, o_ref):
    o_ref[...] = jnp.dot(x_ref[...].astype(jnp.bfloat16), w_ref[...].astype(jnp.bfloat16),
                         preferred_element_type=jnp.float32).astype(o_ref.dtype)


def _kv_proj(mem2d, w, bn):
    m, k = mem2d.shape
    n = w.shape[1]
    return pl.pallas_call(
        _kv_kernel,
        grid=(n // bn,),
        in_specs=[pl.BlockSpec((m, k), lambda j: (0, 0)), pl.BlockSpec((k, bn), lambda j: (0, j))],
        out_specs=pl.BlockSpec((m, bn), lambda j: (0, j)),
        out_shape=jax.ShapeDtypeStruct((m, n), jnp.bfloat16),
        compiler_params=_compiler_params(("parallel",)),
        name="kv_proj",
    )(mem2d, w)


def _t5_bucket(dist):
    max_exact = N_BUCKETS // 2
    d = jnp.maximum(dist, 1).astype(jnp.float32)
    large = max_exact + (jnp.log(d / max_exact) / math.log(MAX_DISTANCE / max_exact)
                         * (N_BUCKETS - max_exact)).astype(jnp.int32)
    large = jnp.minimum(large, N_BUCKETS - 1)
    return jnp.where(dist < max_exact, dist, large)


def _log2(n):
    assert n & (n - 1) == 0
    return n.bit_length() - 1


def _class_slot(c_sub, dil, mcls):
    if dil == 1:
        return REGROUP_STRIDE * (mcls % REGROUP_STRIDE) + mcls // REGROUP_STRIDE
    if dil == REGROUP_STRIDE:
        return c_sub * REGROUP_STRIDE + mcls
    assert dil == N_CLASSES and mcls == 0
    lo = lax.bitwise_and(c_sub, REGROUP_STRIDE - 1)
    hi = lax.shift_right_logical(c_sub, _log2(REGROUP_STRIDE))
    return lo * REGROUP_STRIDE + hi


def _regroup(src_ref, tmp_ref, dst_ref, scale=None):
    seq = src_ref.shape[0]
    quarter = seq // REGROUP_STRIDE
    rows_per_class = seq // N_CLASSES
    for lo in range(REGROUP_STRIDE):
        rows = src_ref[pl.ds(lo, quarter, stride=REGROUP_STRIDE), :]
        tmp_ref[lo * quarter:(lo + 1) * quarter, :] = rows if scale is None else rows * scale
    for lo in range(REGROUP_STRIDE):
        for hi in range(REGROUP_STRIDE):
            slot = lo * REGROUP_STRIDE + hi
            dst_ref[slot * rows_per_class:(slot + 1) * rows_per_class, :] = (
                tmp_ref[pl.ds(lo * quarter + hi, rows_per_class, stride=REGROUP_STRIDE), :])


def _ungroup(read_rows, piece, tmp_ref, dst_ref):
    seq = dst_ref.shape[0]
    quarter = seq // REGROUP_STRIDE
    rows_per_class = seq // N_CLASSES
    for lo in range(REGROUP_STRIDE):
        for hi in range(REGROUP_STRIDE):
            slot = lo * REGROUP_STRIDE + hi
            for i0 in range(0, rows_per_class, piece):
                dst_rows = pl.ds(lo * quarter + hi + REGROUP_STRIDE * i0, piece, stride=REGROUP_STRIDE)
                tmp_ref[dst_rows, :] = read_rows(slot * rows_per_class + i0, piece)
    for lo in range(REGROUP_STRIDE):
        dst_ref[pl.ds(lo, quarter, stride=REGROUP_STRIDE), :] = tmp_ref[lo * quarter:(lo + 1) * quarter, :]


def _dil_attn_kernel(rb_ref, q_ref, k_ref, v_ref, o_ref,
                     qs_ref, ks_ref, vs_ref, tmp_ref, acc_ref, m_ref, l_ref, bias_ref):
    seq = q_ref.shape[0]
    rows_per_class = seq // N_CLASSES
    h = pl.program_id(0)

    _regroup(q_ref, tmp_ref, qs_ref, scale=ATTN_SCALE * LOG2_E)
    _regroup(k_ref, tmp_ref, ks_ref)
    _regroup(v_ref, tmp_ref, vs_ref)

    @pl.when(pl.program_id(1) == 0)
    def _():
        row = lax.broadcasted_iota(jnp.int32, (BLOCK, 2 * BLOCK), 0)
        col = lax.broadcasted_iota(jnp.int32, (BLOCK, 2 * BLOCK), 1)
        for p, dil in enumerate(DILATIONS):
            n_cls = N_CLASSES // dil
            q_rows = BLOCK // n_cls
            k_rows = 2 * BLOCK // n_cls
            q_sub = (row % q_rows) * n_cls + row // q_rows
            k_sub = (col % k_rows) * n_cls + col // k_rows
            for variant, shift in enumerate((N_STEPS, 0)):
                steps = q_sub + shift - k_sub
                valid = (steps >= 0) & (steps <= N_STEPS)
                bucket = _t5_bucket(jnp.maximum(steps, 0) * dil)
                bias = jnp.zeros((BLOCK, 2 * BLOCK), jnp.float32)
                for b in range(N_BUCKETS):
                    bias = jnp.where(bucket == b, rb_ref[b, h], bias)
                bias_ref[2 * p + variant] = jnp.where(valid, bias * LOG2_E, MASK_VALUE)

    for p, dil in enumerate(DILATIONS):
        n_cls = N_CLASSES // dil
        q_rows = BLOCK // n_cls
        k_rows = 2 * BLOCK // n_cls
        blocks_per_sub = seq // dil // BLOCK

        def blocks_body(it, carry, p=p, dil=dil, n_cls=n_cls, q_rows=q_rows, k_rows=k_rows,
                        blocks_per_sub=blocks_per_sub):
            staged = []
            for u in range(ATTN_UNROLL):
                blk = it * ATTN_UNROLL + u
                c_sub = lax.shift_right_logical(blk, _log2(blocks_per_sub))
                n = lax.bitwise_and(blk, blocks_per_sub - 1)
                first = jnp.where(n == 0, 1, 0)
                k_back = q_rows - first * q_rows
                q_offs, k_offs = [], []
                for mcls in range(n_cls):
                    base = _class_slot(c_sub, dil, mcls) * rows_per_class + n * q_rows
                    q_offs.append(pl.multiple_of(base, SUBLANES))
                    k_offs.append(pl.multiple_of(base - k_back, SUBLANES))
                qb = jnp.concatenate([qs_ref[pl.ds(o, q_rows), :] for o in q_offs], axis=0)
                kb = jnp.concatenate([ks_ref[pl.ds(o, k_rows), :] for o in k_offs], axis=0)
                s = lax.dot_general(qb.astype(jnp.bfloat16), kb.astype(jnp.bfloat16), _NT_DIMS,
                                    preferred_element_type=jnp.float32)
                staged.append((q_offs, k_offs, s + bias_ref[2 * p + first]))
            softmaxed = []
            for q_offs, k_offs, s in staged:
                m_blk = jnp.max(s, axis=-1, keepdims=True)
                e = jnp.exp2(s - m_blk)
                l_blk = jnp.sum(e, axis=-1, keepdims=True)
                softmaxed.append((q_offs, k_offs, m_blk, l_blk, e.astype(jnp.bfloat16)))
            for q_offs, k_offs, m_blk, l_blk, e in softmaxed:
                vb = jnp.concatenate([vs_ref[pl.ds(o, k_rows), :] for o in k_offs], axis=0)
                pv = jnp.dot(e, vb.astype(jnp.bfloat16), preferred_element_type=jnp.float32)
                for mcls in range(n_cls):
                    rows = pl.ds(q_offs[mcls], q_rows)
                    sl = slice(mcls * q_rows, (mcls + 1) * q_rows)
                    m_ref[p, rows, :] = jnp.broadcast_to(m_blk[sl], (q_rows, HEAD_DIM))
                    l_ref[p, rows, :] = jnp.broadcast_to(l_blk[sl], (q_rows, HEAD_DIM))
                    acc_ref[p, rows, :] = pv[sl]
            return carry

        lax.fori_loop(0, dil * blocks_per_sub // ATTN_UNROLL, blocks_body, 0)

    def combined(start, size):
        rows = slice(start, start + size)
        maxes = [m_ref[p, rows, :] for p in range(len(DILATIONS))]
        m_all = jnp.maximum(jnp.maximum(maxes[0], maxes[1]), maxes[2])
        den = jnp.zeros((size, HEAD_DIM), jnp.float32)
        num = jnp.zeros((size, HEAD_DIM), jnp.float32)
        for p in range(len(DILATIONS)):
            w = jnp.exp2(maxes[p] - m_all)
            den = den + w * l_ref[p, rows, :]
            num = num + w * acc_ref[p, rows, :]
        return num / den

    _ungroup(combined, COMBINE_ROWS, tmp_ref, o_ref)


def _dilated_attention(hcat, rel_bias, batch, seq):
    hcat3 = hcat.reshape(batch, seq, hcat.shape[-1])
    blk = (None, seq, HEAD_DIM)
    rows = pltpu.VMEM((seq, HEAD_DIM), jnp.float32)
    per_pattern = pltpu.VMEM((len(DILATIONS), seq, HEAD_DIM), jnp.float32)
    return pl.pallas_call(
        _dil_attn_kernel,
        grid=(N_HEADS_DIL, batch),
        in_specs=[pl.BlockSpec(memory_space=pltpu.SMEM),
                  pl.BlockSpec(blk, lambda h, b: (b, 0, h)),
                  pl.BlockSpec(blk, lambda h, b: (b, 0, N_HEADS_DIL + h)),
                  pl.BlockSpec(blk, lambda h, b: (b, 0, 2 * N_HEADS_DIL + h))],
        out_specs=pl.BlockSpec(blk, lambda h, b: (b, 0, h)),
        out_shape=jax.ShapeDtypeStruct((batch, seq, D_DIL), jnp.float32),
        scratch_shapes=[rows] * 4 + [per_pattern] * 3
        + [pltpu.VMEM((2 * len(DILATIONS), BLOCK, 2 * BLOCK), jnp.float32)],
        compiler_params=_compiler_params(("arbitrary", "arbitrary")),
        name="dilated_attn",
    )(rel_bias, hcat3, hcat3, hcat3)


def _mix_kernel(x_ref, odil_ref, u_ref, v_ref, qm_ref, kv_ref, wsp_ref, bsp_ref, sgg_ref, sgb_ref,
                wout_ref, g1_ref, b1_ref, o_ref, ob_ref):
    tm = x_ref.shape[0]
    sub = tm // ROW_SPLIT
    r_i = lax.broadcasted_iota(jnp.int32, (SG_CHUNK, SG_CHUNK), 0)
    c_i = lax.broadcasted_iota(jnp.int32, (SG_CHUNK, SG_CHUNK), 1)
    causal = r_i >= c_i
    w_sp = [jnp.where(causal, wsp_ref[g], 0.0).astype(jnp.bfloat16) for g in range(N_GROUPS_SG)]
    bsp = bsp_ref[...]

    def mixer_outputs(rows):
        u = jax.nn.gelu(u_ref[rows, :])
        v = _layer_norm(jax.nn.gelu(v_ref[rows, :]), sgg_ref[...], sgb_ref[...]).astype(jnp.bfloat16)
        sg_rows = []
        for ci in range(sub // SG_CHUNK):
            rs = slice(ci * SG_CHUNK, (ci + 1) * SG_CHUNK)
            cols = []
            for g in range(N_GROUPS_SG):
                cs = slice(g * SG_CHUNK, (g + 1) * SG_CHUNK)
                mixed = jnp.dot(w_sp[g], v[rs, cs], preferred_element_type=jnp.float32)
                cols.append((u[rs, cs] * (mixed + bsp[:, g:g + 1])).astype(jnp.bfloat16))
            sg_rows.append(jnp.concatenate(cols, axis=1))
        o_sg = jnp.concatenate(sg_rows, axis=0)

        qm = qm_ref[rows, :].astype(jnp.bfloat16)
        mem_cols = []
        for hh in range(N_HEADS_MEM):
            cs = slice(hh * HEAD_DIM, (hh + 1) * HEAD_DIM)
            kh = kv_ref[:, hh * HEAD_DIM:(hh + 1) * HEAD_DIM]
            vh = kv_ref[:, D_MEM_ATTN + hh * HEAD_DIM:D_MEM_ATTN + (hh + 1) * HEAD_DIM]
            s = lax.dot_general(qm[:, cs], kh, _NT_DIMS,
                                preferred_element_type=jnp.float32) * ATTN_SCALE
            s_max = jnp.max(s, axis=-1, keepdims=True)
            e = jnp.exp(s - s_max)
            den = jnp.sum(e, axis=-1, keepdims=True)
            o = jnp.dot(e.astype(jnp.bfloat16), vh, preferred_element_type=jnp.float32) / den
            mem_cols.append(o.astype(jnp.bfloat16))
        return jnp.concatenate([odil_ref[rows, :].astype(jnp.bfloat16), o_sg] + mem_cols, axis=1)

    row_parts = [slice(k * sub, (k + 1) * sub) for k in range(ROW_SPLIT)]
    mix_ins = [mixer_outputs(rows) for rows in row_parts]
    mixes = [jnp.dot(mi, wout_ref[...], preferred_element_type=jnp.float32) for mi in mix_ins]
    for rows, mix in zip(row_parts, mixes):
        x1 = _layer_norm(DEEPNORM_ALPHA * x_ref[rows, :] + mix, g1_ref[...], b1_ref[...])
        o_ref[rows, :] = x1
        ob_ref[rows, :] = x1.astype(jnp.bfloat16)


def _mix(x2d, o_dil2d, hcat, kv, w_sp, b_sp_t, sg_g, sg_b, w_out, g1, b1, seq, tm):
    m, d_model = x2d.shape
    tiles_per_batch = seq // tm
    n_mem = kv.shape[1]
    u_blk = 3 * D_DIL // D_SG
    const = lambda i: (0, 0)
    return pl.pallas_call(
        _mix_kernel,
        grid=(m // tm,),
        in_specs=[pl.BlockSpec((tm, d_model), lambda i: (i, 0)),
                  pl.BlockSpec((tm, D_DIL), lambda i: (i, 0)),
                  pl.BlockSpec((tm, D_SG), lambda i: (i, u_blk)),
                  pl.BlockSpec((tm, D_SG), lambda i: (i, u_blk + 1)),
                  pl.BlockSpec((tm, D_MEM_ATTN), lambda i: (i, u_blk + 2)),
                  pl.BlockSpec((None, n_mem, 2 * D_MEM_ATTN), lambda i: (i // tiles_per_batch, 0, 0)),
                  _resident((N_GROUPS_SG, SG_CHUNK, SG_CHUNK), lambda i: (0, 0, 0)),
                  _resident((SG_CHUNK, N_GROUPS_SG), const),
                  _resident((1, D_SG), const),
                  _resident((1, D_SG), const),
                  _resident((d_model, d_model), const),
                  _resident((1, d_model), const),
                  _resident((1, d_model), const)],
        out_specs=[pl.BlockSpec((tm, d_model), lambda i: (i, 0)),
                   pl.BlockSpec((tm, d_model), lambda i: (i, 0))],
        out_shape=[jax.ShapeDtypeStruct((m, d_model), jnp.float32),
                   jax.ShapeDtypeStruct((m, d_model), jnp.bfloat16)],
        compiler_params=_compiler_params(("parallel",)),
        name="mix_ln1",
    )(x2d, o_dil2d, hcat, hcat, hcat, kv, w_sp, b_sp_t, sg_g, sg_b, w_out, g1, b1)


FFN_COL_SPLIT = 2


def _ffn_hidden_kernel(x_ref, wg_ref, wu_ref, side_ref, h_ref, side_out_ref, wgb_ref, wub_ref):
    @pl.when(pl.program_id(1) == 0)
    def _():
        wgb_ref[...] = wg_ref[...].astype(jnp.bfloat16)
        wub_ref[...] = wu_ref[...].astype(jnp.bfloat16)

    side_out_ref[...] = side_ref[...].astype(jnp.bfloat16)
    x = x_ref[...]
    bn = h_ref.shape[1]
    part = bn // FFN_COL_SPLIT
    for k in range(FFN_COL_SPLIT):
        cs = slice(k * part, (k + 1) * part)
        gate = jnp.dot(x, wgb_ref[:, cs], preferred_element_type=jnp.float32)
        up = jnp.dot(x, wub_ref[:, cs], preferred_element_type=jnp.float32)
        h_ref[:, cs] = (jax.nn.silu(gate) * up).astype(h_ref.dtype)


def _ffn_hidden(x1b, wg, wu, w_side, bm, bn):
    m, d_model = x1b.shape
    d_ff = wg.shape[1]
    grid = (d_ff // bn, m // bm)
    side_rows = w_side.shape[0] // (grid[0] * grid[1])
    side_spec = pl.BlockSpec((side_rows, w_side.shape[1]), lambda j, i: (j * grid[1] + i, 0))
    return pl.pallas_call(
        _ffn_hidden_kernel,
        grid=grid,
        in_specs=[pl.BlockSpec((bm, d_model), lambda j, i: (i, 0)),
                  pl.BlockSpec((d_model, bn), lambda j, i: (0, j)),
                  pl.BlockSpec((d_model, bn), lambda j, i: (0, j)),
                  side_spec],
        out_specs=[pl.BlockSpec((bm, bn), lambda j, i: (i, j)), side_spec],
        out_shape=[jax.ShapeDtypeStruct((m, d_ff), jnp.bfloat16),
                   jax.ShapeDtypeStruct(w_side.shape, jnp.bfloat16)],
        scratch_shapes=[pltpu.VMEM((d_model, bn), jnp.bfloat16),
                        pltpu.VMEM((d_model, bn), jnp.bfloat16)],
        compiler_params=_compiler_params(("parallel", "arbitrary")),
        name="ffn_hidden",
    )(x1b, wg, wu, w_side)


def _ffn_down_kernel(h_ref, x_ref, wd_ref, g2_ref, b2_ref, o_ref):
    f = jnp.dot(h_ref[...], wd_ref[...], preferred_element_type=jnp.float32)
    o_ref[...] = _layer_norm(DEEPNORM_ALPHA * x_ref[...] + f, g2_ref[...], b2_ref[...])


def _ffn_down(hid, x1, wd, g2, b2, tm):
    m, d_ff = hid.shape
    d_model = wd.shape[1]
    const = lambda i: (0, 0)
    return pl.pallas_call(
        _ffn_down_kernel,
        grid=(m // tm,),
        in_specs=[pl.BlockSpec((tm, d_ff), lambda i: (i, 0)),
                  pl.BlockSpec((tm, d_model), lambda i: (i, 0)),
                  _resident((d_ff, d_model), const),
                  _resident((1, d_model), const),
                  _resident((1, d_model), const)],
        out_specs=pl.BlockSpec((tm, d_model), lambda i: (i, 0)),
        out_shape=jax.ShapeDtypeStruct((m, d_model), jnp.float32),
        compiler_params=_compiler_params(("parallel",)),
        name="ffn_down_ln2",
    )(hid, x1, wd, g2, b2)


def kernel(x, mem, w_in, rel_bias, sg_ln_g, sg_ln_b, w_spatial, b_spatial, w_mem_kv, w_out,
           ln1_g, ln1_b, w_gate, w_up, w_down, ln2_g, ln2_b):
    batch, seq, d_model = x.shape
    n_mem = mem.shape[1]
    depth = w_in.shape[0]
    h2d = x.reshape(batch * seq, d_model)
    for l in range(depth):
        hcat, w_out_b = _proj(h2d, w_in[l].astype(jnp.bfloat16), w_out[l], bm=1024, bn=1152)
        kv = _kv_proj(mem.reshape(batch * n_mem, d_model), w_mem_kv[l], bn=D_MEM_ATTN)
        o_dil = _dilated_attention(hcat, rel_bias, batch, seq)
        x1, x1b = _mix(h2d, o_dil.reshape(batch * seq, D_DIL), hcat,
                       kv.reshape(batch, n_mem, 2 * D_MEM_ATTN), w_spatial[l], b_spatial[l].T,
                       sg_ln_g[l][None], sg_ln_b[l][None], w_out_b,
                       ln1_g[l][None], ln1_b[l][None], seq, tm=512)
        hid, w_down_b = _ffn_hidden(x1b, w_gate[l], w_up[l], w_down[l], bm=1024, bn=512)
        h2d = _ffn_down(hid, x1, w_down_b, ln2_g[l][None], ln2_b[l][None], tm=512)
    return h2d.reshape(batch, seq, d_model)
```

```python
import math

import jax
import jax.numpy as jnp
from jax import lax
from jax.experimental import pallas as pl
from jax.experimental.pallas import tpu as pltpu

D_MODEL = 2048
HEAD_DIM = 128
N_HEADS_DIL = 8
D_DIL = N_HEADS_DIL * HEAD_DIM
DILATIONS = (16, 4, 1)
N_STEPS = 128
BLOCK = 128
N_GROUPS_SG = 4
SG_CHUNK = 128
D_SG = N_GROUPS_SG * SG_CHUNK
N_HEADS_MEM = 4
D_MEM_ATTN = N_HEADS_MEM * HEAD_DIM
N_BUCKETS = 32
MAX_DISTANCE = 2048
DEEPNORM_ALPHA = 2.0 ** 0.25
LN_EPS = 1e-5
ATTN_SCALE = HEAD_DIM ** -0.5
LOG2_E = math.log2(math.e)
MASK_VALUE = -1e30
N_CLASSES = 16
REGROUP_STRIDE = 4
SUBLANES = 8
ATTN_UNROLL = 8
COMBINE_ROWS = 64
ROW_SPLIT = 2

VMEM_LIMIT_BYTES = 58 * 1024 * 1024

_NT_DIMS = (((1,), (1,)), ((), ()))


def _layer_norm(y, g, b):
    mu = jnp.mean(y, axis=-1, keepdims=True)
    d = y - mu
    var = jnp.mean(d * d, axis=-1, keepdims=True)
    return d * lax.rsqrt(var + LN_EPS) * g + b


def _compiler_params(semantics):
    return pltpu.CompilerParams(dimension_semantics=semantics, vmem_limit_bytes=VMEM_LIMIT_BYTES)


def _resident(block_shape, index_map):
    return pl.BlockSpec(block_shape, index_map, pipeline_mode=pl.Buffered(1))


def _proj_kernel(x_ref, w_ref, side_ref, o_ref, side_out_ref, xb_ref):
    @pl.when(pl.program_id(1) == 0)
    def _():
        xb_ref[...] = x_ref[...].astype(jnp.bfloat16)
        side_out_ref[...] = side_ref[...].astype(jnp.bfloat16)

    o_ref[...] = jnp.dot(xb_ref[...], w_ref[...], preferred_element_type=jnp.float32)


def _proj(x2d, w_bf16, w_side, bm, bn):
    m, k = x2d.shape
    n = w_bf16.shape[1]
    grid = (m // bm, n // bn)
    side_rows = w_side.shape[0] // grid[0]
    side_spec = pl.BlockSpec((side_rows, w_side.shape[1]), lambda i, j: (i, 0))
    return pl.pallas_call(
        _proj_kernel,
        grid=grid,
        in_specs=[pl.BlockSpec((bm, k), lambda i, j: (i, 0)),
                  pl.BlockSpec((k, bn), lambda i, j: (0, j)),
                  side_spec],
        out_specs=[pl.BlockSpec((bm, bn), lambda i, j: (i, j)), side_spec],
        out_shape=[jax.ShapeDtypeStruct((m, n), jnp.float32),
                   jax.ShapeDtypeStruct(w_side.shape, jnp.bfloat16)],
        scratch_shapes=[pltpu.VMEM((bm, k), jnp.bfloat16)],
        compiler_params=_compiler_params(("parallel", "arbitrary")),
        name="proj",
    )(x2d, w_bf16, w_side)


def _kv_kernel(x_ref, w_ref, o_ref):
    o_ref[...] = jnp.dot(x_ref[...].astype(jnp.bfloat16), w_ref[...].astype(jnp.bfloat16),
                         preferred_element_type=jnp.float32).astype(o_ref.dtype)


def _kv_proj(mem2d, w, bn):
    m, k = mem2d.shape
    n = w.shape[1]
    return pl.pallas_call(
        _kv_kernel,
        grid=(n // bn,),
        in_specs=[pl.BlockSpec((m, k), lambda j: (0, 0)), pl.BlockSpec((k, bn), lambda j: (0, j))],
        out_specs=pl.BlockSpec((m, bn), lambda j: (0, j)),
        out_shape=jax.ShapeDtypeStruct((m, n), jnp.bfloat16),
        compiler_params=_compiler_params(("parallel",)),
        name="kv_proj",
    )(mem2d, w)


def _t5_bucket(dist):
    max_exact = N_BUCKETS // 2
    d = jnp.maximum(dist, 1).astype(jnp.float32)
    large = max_exact + (jnp.log(d / max_exact) / math.log(MAX_DISTANCE / max_exact)
                         * (N_BUCKETS - max_exact)).astype(jnp.int32)
    large = jnp.minimum(large, N_BUCKETS - 1)
    return jnp.where(dist < max_exact, dist, large)


def _log2(n):
    assert n & (n - 1) == 0
    return n.bit_length() - 1


def _class_slot(c_sub, dil, mcls):
    if dil == 1:
        return REGROUP_STRIDE * (mcls % REGROUP_STRIDE) + mcls // REGROUP_STRIDE
    if dil == REGROUP_STRIDE:
        return c_sub * REGROUP_STRIDE + mcls
    assert dil == N_CLASSES and mcls == 0
    lo = lax.bitwise_and(c_sub, REGROUP_STRIDE - 1)
    hi = lax.shift_right_logical(c_sub, _log2(REGROUP_STRIDE))
    return lo * REGROUP_STRIDE + hi


def _regroup(src_ref, tmp_ref, dst_ref, scale=None):
    seq = src_ref.shape[0]
    quarter = seq // REGROUP_STRIDE
    rows_per_class = seq // N_CLASSES
    for lo in range(REGROUP_STRIDE):
        rows = src_ref[pl.ds(lo, quarter, stride=REGROUP_STRIDE), :]
        tmp_ref[lo * quarter:(lo + 1) * quarter, :] = rows if scale is None else rows * scale
    for lo in range(REGROUP_STRIDE):
        for hi in range(REGROUP_STRIDE):
            slot = lo * REGROUP_STRIDE + hi
            dst_ref[slot * rows_per_class:(slot + 1) * rows_per_class, :] = (
                tmp_ref[pl.ds(lo * quarter + hi, rows_per_class, stride=REGROUP_STRIDE), :])


def _ungroup(read_rows, piece, tmp_ref, dst_ref):
    seq = dst_ref.shape[0]
    quarter = seq // REGROUP_STRIDE
    rows_per_class = seq // N_CLASSES
    for lo in range(REGROUP_STRIDE):
        for hi in range(REGROUP_STRIDE):
            slot = lo * REGROUP_STRIDE + hi
            for i0 in range(0, rows_per_class, piece):
                dst_rows = pl.ds(lo * quarter + hi + REGROUP_STRIDE * i0, piece, stride=REGROUP_STRIDE)
                tmp_ref[dst_rows, :] = read_rows(slot * rows_per_class + i0, piece)
    for lo in range(REGROUP_STRIDE):
        dst_ref[pl.ds(lo, quarter, stride=REGROUP_STRIDE), :] = tmp_ref[lo * quarter:(lo + 1) * quarter, :]


def _dil_attn_kernel(rb_ref, q_ref, k_ref, v_ref, o_ref,
                     qs_ref, ks_ref, vs_ref, tmp_ref, acc_ref, m_ref, l_ref, bias_ref):
    seq = q_ref.shape[0]
    rows_per_class = seq // N_CLASSES
    h = pl.program_id(0)

    _regroup(q_ref, tmp_ref, qs_ref, scale=ATTN_SCALE * LOG2_E)
    _regroup(k_ref, tmp_ref, ks_ref)
    _regroup(v_ref, tmp_ref, vs_ref)

    @pl.when(pl.program_id(1) == 0)
    def _():
        row = lax.broadcasted_iota(jnp.int32, (BLOCK, 2 * BLOCK), 0)
        col = lax.broadcasted_iota(jnp.int32, (BLOCK, 2 * BLOCK), 1)
        for p, dil in enumerate(DILATIONS):
            n_cls = N_CLASSES // dil
            q_rows = BLOCK // n_cls
            k_rows = 2 * BLOCK // n_cls
            q_sub = (row % q_rows) * n_cls + row // q_rows
            k_sub = (col % k_rows) * n_cls + col // k_rows
            for variant, shift in enumerate((N_STEPS, 0)):
                steps = q_sub + shift - k_sub
                valid = (steps >= 0) & (steps <= N_STEPS)
                bucket = _t5_bucket(jnp.maximum(steps, 0) * dil)
                bias = jnp.zeros((BLOCK, 2 * BLOCK), jnp.float32)
                for b in range(N_BUCKETS):
                    bias = jnp.where(bucket == b, rb_ref[b, h], bias)
                bias_ref[2 * p + variant] = jnp.where(valid, bias * LOG2_E, MASK_VALUE)

    for p, dil in enumerate(DILATIONS):
        n_cls = N_CLASSES // dil
        q_rows = BLOCK // n_cls
        k_rows = 2 * BLOCK // n_cls
        blocks_per_sub = seq // dil // BLOCK

        def blocks_body(it, carry, p=p, dil=dil, n_cls=n_cls, q_rows=q_rows, k_rows=k_rows,
                        blocks_per_sub=blocks_per_sub):
            staged = []
            for u in range(ATTN_UNROLL):
                blk = it * ATTN_UNROLL + u
                c_sub = lax.shift_right_logical(blk, _log2(blocks_per_sub))
                n = lax.bitwise_and(blk, blocks_per_sub - 1)
                first = jnp.where(n == 0, 1, 0)
                k_back = q_rows - first * q_rows
                q_offs, k_offs = [], []
                for mcls in range(n_cls):
                    base = _class_slot(c_sub, dil, mcls) * rows_per_class + n * q_rows
                    q_offs.append(pl.multiple_of(base, SUBLANES))
                    k_offs.append(pl.multiple_of(base - k_back, SUBLANES))
                qb = jnp.concatenate([qs_ref[pl.ds(o, q_rows), :] for o in q_offs], axis=0)
                kb = jnp.concatenate([ks_ref[pl.ds(o, k_rows), :] for o in k_offs], axis=0)
                s = lax.dot_general(qb.astype(jnp.bfloat16), kb.astype(jnp.bfloat16), _NT_DIMS,
                                    preferred_element_type=jnp.float32)
                staged.append((q_offs, k_offs, s + bias_ref[2 * p + first]))
            softmaxed = []
            for q_offs, k_offs, s in staged:
                m_blk = jnp.max(s, axis=-1, keepdims=True)
                e = jnp.exp2(s - m_blk)
                l_blk = jnp.sum(e, axis=-1, keepdims=True)
                softmaxed.append((q_offs, k_offs, m_blk, l_blk, e.astype(jnp.bfloat16)))
            for q_offs, k_offs, m_blk, l_blk, e in softmaxed:
                vb = jnp.concatenate([vs_ref[pl.ds(o, k_rows), :] for o in k_offs], axis=0)
                pv = jnp.dot(e, vb.astype(jnp.bfloat16), preferred_element_type=jnp.float32)
                for mcls in range(n_cls):
                    rows = pl.ds(q_offs[mcls], q_rows)
                    sl = slice(mcls * q_rows, (mcls + 1) * q_rows)
                    m_ref[p, rows, :] = jnp.broadcast_to(m_blk[sl], (q_rows, HEAD_DIM))
                    l_ref[p, rows, :] = jnp.broadcast_to(l_blk[sl], (q_rows, HEAD_DIM))
                    acc_ref[p, rows, :] = pv[sl]
            return carry

        lax.fori_loop(0, dil * blocks_per_sub // ATTN_UNROLL, blocks_body, 0)

    def combined(start, size):
        rows = slice(start, start + size)
        maxes = [m_ref[p, rows, :] for p in range(len(DILATIONS))]
        m_all = jnp.maximum(jnp.maximum(maxes[0], maxes[1]), maxes[2])
        den = jnp.zeros((size, HEAD_DIM), jnp.float32)
        num = jnp.zeros((size, HEAD_DIM), jnp.float32)
        for p in range(len(DILATIONS)):
            w = jnp.exp2(maxes[p] - m_all)
            den = den + w * l_ref[p, rows, :]
            num = num + w * acc_ref[p, rows, :]
        return num / den

    _ungroup(combined, COMBINE_ROWS, tmp_ref, o_ref)


def _dilated_attention(hcat, rel_bias, batch, seq):
    hcat3 = hcat.reshape(batch, seq, hcat.shape[-1])
    blk = (None, seq, HEAD_DIM)
    rows = pltpu.VMEM((seq, HEAD_DIM), jnp.float32)
    per_pattern = pltpu.VMEM((len(DILATIONS), seq, HEAD_DIM), jnp.float32)
    return pl.pallas_call(
        _dil_attn_kernel,
        grid=(N_HEADS_DIL, batch),
        in_specs=[pl.BlockSpec(memory_space=pltpu.SMEM),
                  pl.BlockSpec(blk, lambda h, b: (b, 0, h)),
                  pl.BlockSpec(blk, lambda h, b: (b, 0, N_HEADS_DIL + h)),
                  pl.BlockSpec(blk, lambda h, b: (b, 0, 2 * N_HEADS_DIL + h))],
        out_specs=pl.BlockSpec(blk, lambda h, b: (b, 0, h)),
        out_shape=jax.ShapeDtypeStruct((batch, seq, D_DIL), jnp.float32),
        scratch_shapes=[rows] * 4 + [per_pattern] * 3
        + [pltpu.VMEM((2 * len(DILATIONS), BLOCK, 2 * BLOCK), jnp.float32)],
        compiler_params=_compiler_params(("arbitrary", "arbitrary")),
        name="dilated_attn",
    )(rel_bias, hcat3, hcat3, hcat3)


def _mix_kernel(x_ref, odil_ref, u_ref, v_ref, qm_ref, kv_ref, wsp_ref, bsp_ref, sgg_ref, sgb_ref,
                wout_ref, g1_ref, b1_ref, o_ref, ob_ref):
    tm = x_ref.shape[0]
    sub = tm // ROW_SPLIT
    r_i = lax.broadcasted_iota(jnp.int32, (SG_CHUNK, SG_CHUNK), 0)
    c_i = lax.broadcasted_iota(jnp.int32, (SG_CHUNK, SG_CHUNK), 1)
    causal = r_i >= c_i
    w_sp = [jnp.where(causal, wsp_ref[g], 0.0).astype(jnp.bfloat16) for g in range(N_GROUPS_SG)]
    bsp = bsp_ref[...]

    def mixer_outputs(rows):
        u = jax.nn.gelu(u_ref[rows, :])
        v = _layer_norm(jax.nn.gelu(v_ref[rows, :]), sgg_ref[...], sgb_ref[...]).astype(jnp.bfloat16)
        sg_rows = []
        for ci in range(sub // SG_CHUNK):
            rs = slice(ci * SG_CHUNK, (ci + 1) * SG_CHUNK)
            cols = []
            for g in range(N_GROUPS_SG):
                cs = slice(g * SG_CHUNK, (g + 1) * SG_CHUNK)
                mixed = jnp.dot(w_sp[g], v[rs, cs], preferred_element_type=jnp.float32)
                cols.append((u[rs, cs] * (mixed + bsp[:, g:g + 1])).astype(jnp.bfloat16))
            sg_rows.append(jnp.concatenate(cols, axis=1))
        o_sg = jnp.concatenate(sg_rows, axis=0)

        qm = qm_ref[rows, :].astype(jnp.bfloat16)
        mem_cols = []
        for hh in range(N_HEADS_MEM):
            cs = slice(hh * HEAD_DIM, (hh + 1) * HEAD_DIM)
            kh = kv_ref[:, hh * HEAD_DIM:(hh + 1) * HEAD_DIM]
            vh = kv_ref[:, D_MEM_ATTN + hh * HEAD_DIM:D_MEM_ATTN + (hh + 1) * HEAD_DIM]
            s = lax.dot_general(qm[:, cs], kh, _NT_DIMS,
                                preferred_element_type=jnp.float32) * ATTN_SCALE
            s_max = jnp.max(s, axis=-1, keepdims=True)
            e = jnp.exp(s - s_max)
            den = jnp.sum(e, axis=-1, keepdims=True)
            o = jnp.dot(e.astype(jnp.bfloat16), vh, preferred_element_type=jnp.float32) / den
            mem_cols.append(o.astype(jnp.bfloat16))
        return jnp.concatenate([odil_ref[rows, :].astype(jnp.bfloat16), o_sg] + mem_cols, axis=1)

    row_parts = [slice(k * sub, (k + 1) * sub) for k in range(ROW_SPLIT)]
    mix_ins = [mixer_outputs(rows) for rows in row_parts]
    mixes = [jnp.dot(mi, wout_ref[...], preferred_element_type=jnp.float32) for mi in mix_ins]
    for rows, mix in zip(row_parts, mixes):
        x1 = _layer_norm(DEEPNORM_ALPHA * x_ref[rows, :] + mix, g1_ref[...], b1_ref[...])
        o_ref[rows, :] = x1
        ob_ref[rows, :] = x1.astype(jnp.bfloat16)


def _mix(x2d, o_dil2d, hcat, kv, w_sp, b_sp_t, sg_g, sg_b, w_out, g1, b1, seq, tm):
    m, d_model = x2d.shape
    tiles_per_batch = seq // tm
    n_mem = kv.shape[1]
    u_blk = 3 * D_DIL // D_SG
    const = lambda i: (0, 0)
    return pl.pallas_call(
        _mix_kernel,
        grid=(m // tm,),
        in_specs=[pl.BlockSpec((tm, d_model), lambda i: (i, 0)),
                  pl.BlockSpec((tm, D_DIL), lambda i: (i, 0)),
                  pl.BlockSpec((tm, D_SG), lambda i: (i, u_blk)),
                  pl.BlockSpec((tm, D_SG), lambda i: (i, u_blk + 1)),
                  pl.BlockSpec((tm, D_MEM_ATTN), lambda i: (i, u_blk + 2)),
                  pl.BlockSpec((None, n_mem, 2 * D_MEM_ATTN), lambda i: (i // tiles_per_batch, 0, 0)),
                  _resident((N_GROUPS_SG, SG_CHUNK, SG_CHUNK), lambda i: (0, 0, 0)),
                  _resident((SG_CHUNK, N_GROUPS_SG), const),
                  _resident((1, D_SG), const),
                  _resident((1, D_SG), const),
                  _resident((d_model, d_model), const),
                  _resident((1, d_model), const),
                  _resident((1, d_model), const)],
        out_specs=[pl.BlockSpec((tm, d_model), lambda i: (i, 0)),
                   pl.BlockSpec((tm, d_model), lambda i: (i, 0))],
        out_shape=[jax.ShapeDtypeStruct((m, d_model), jnp.float32),
                   jax.ShapeDtypeStruct((m, d_model), jnp.bfloat16)],
        compiler_params=_compiler_params(("parallel",)),
        name="mix_ln1",
    )(x2d, o_dil2d, hcat, hcat, hcat, kv, w_sp, b_sp_t, sg_g, sg_b, w_out, g1, b1)


FFN_COL_SPLIT = 2


def _ffn_hidden_kernel(x_ref, wg_ref, wu_ref, side_ref, h_ref, side_out_ref, wgb_ref, wub_ref):
    @pl.when(pl.program_id(1) == 0)
    def _():
        wgb_ref[...] = wg_ref[...].astype(jnp.bfloat16)
        wub_ref[...] = wu_ref[...].astype(jnp.bfloat16)

    side_out_ref[...] = side_ref[...].astype(jnp.bfloat16)
    x = x_ref[...]
    bn = h_ref.shape[1]
    part = bn // FFN_COL_SPLIT
    for k in range(FFN_COL_SPLIT):
        cs = slice(k * part, (k + 1) * part)
        gate = jnp.dot(x, wgb_ref[:, cs], preferred_element_type=jnp.float32)
        up = jnp.dot(x, wub_ref[:, cs], preferred_element_type=jnp.float32)
        h_ref[:, cs] = (jax.nn.silu(gate) * up).astype(h_ref.dtype)


def _ffn_hidden(x1b, wg, wu, w_side, bm, bn):
    m, d_model = x1b.shape
    d_ff = wg.shape[1]
    grid = (d_ff // bn, m // bm)
    side_rows = w_side.shape[0] // (grid[0] * grid[1])
    side_spec = pl.BlockSpec((side_rows, w_side.shape[1]), lambda j, i: (j * grid[1] + i, 0))
    return pl.pallas_call(
        _ffn_hidden_kernel,
        grid=grid,
        in_specs=[pl.BlockSpec((bm, d_model), lambda j, i: (i, 0)),
                  pl.BlockSpec((d_model, bn), lambda j, i: (0, j)),
                  pl.BlockSpec((d_model, bn), lambda j, i: (0, j)),
                  side_spec],
        out_specs=[pl.BlockSpec((bm, bn), lambda j, i: (i, j)), side_spec],
        out_shape=[jax.ShapeDtypeStruct((m, d_ff), jnp.bfloat16),
                   jax.ShapeDtypeStruct(w_side.shape, jnp.bfloat16)],
        scratch_shapes=[pltpu.VMEM((d_model, bn), jnp.bfloat16),
                        pltpu.VMEM((d_model, bn), jnp.bfloat16)],
        compiler_params=_compiler_params(("parallel", "arbitrary")),
        name="ffn_hidden",
    )(x1b, wg, wu, w_side)


def _ffn_down_kernel(h_ref, x_ref, wd_ref, g2_ref, b2_ref, o_ref, f_ref):
    sub = h_ref.shape[0] // ROW_SPLIT
    row_parts = [slice(k * sub, (k + 1) * sub) for k in range(ROW_SPLIT)]
    for rows in row_parts:
        f_ref[rows, :] = jnp.dot(h_ref[rows, :], wd_ref[...], preferred_element_type=jnp.float32)
    for rows in row_parts:
        o_ref[rows, :] = _layer_norm(DEEPNORM_ALPHA * x_ref[rows, :] + f_ref[rows, :],
                                     g2_ref[...], b2_ref[...])


def _ffn_down(hid, x1, wd, g2, b2, tm):
    m, d_ff = hid.shape
    d_model = wd.shape[1]
    const = lambda i: (0, 0)
    return pl.pallas_call(
        _ffn_down_kernel,
        grid=(m // tm,),
        in_specs=[pl.BlockSpec((tm, d_ff), lambda i: (i, 0)),
                  pl.BlockSpec((tm, d_model), lambda i: (i, 0)),
                  _resident((d_ff, d_model), const),
                  _resident((1, d_model), const),
                  _resident((1, d_model), const)],
        out_specs=pl.BlockSpec((tm, d_model), lambda i: (i, 0)),
        out_shape=jax.ShapeDtypeStruct((m, d_model), jnp.float32),
        scratch_shapes=[pltpu.VMEM((tm, d_model), jnp.float32)],
        compiler_params=_compiler_params(("parallel",)),
        name="ffn_down_ln2",
    )(hid, x1, wd, g2, b2)


def kernel(x, mem, w_in, rel_bias, sg_ln_g, sg_ln_b, w_spatial, b_spatial, w_mem_kv, w_out,
           ln1_g, ln1_b, w_gate, w_up, w_down, ln2_g, ln2_b):
    batch, seq, d_model = x.shape
    n_mem = mem.shape[1]
    depth = w_in.shape[0]
    h2d = x.reshape(batch * seq, d_model)
    for l in range(depth):
        hcat, w_out_b = _proj(h2d, w_in[l].astype(jnp.bfloat16), w_out[l], bm=1024, bn=1536)
        kv = _kv_proj(mem.reshape(batch * n_mem, d_model), w_mem_kv[l], bn=D_MEM_ATTN)
        o_dil = _dilated_attention(hcat, rel_bias, batch, seq)
        x1, x1b = _mix(h2d, o_dil.reshape(batch * seq, D_DIL), hcat,
                       kv.reshape(batch, n_mem, 2 * D_MEM_ATTN), w_spatial[l], b_spatial[l].T,
                       sg_ln_g[l][None], sg_ln_b[l][None], w_out_b,
                       ln1_g[l][None], ln1_b[l][None], seq, tm=512)
        hid, w_down_b = _ffn_hidden(x1b, w_gate[l], w_up[l], w_down[l], bm=2048, bn=512)
        h2d = _ffn_down(hid, x1, w_down_b, ln2_g[l][None], ln2_b[l][None], tm=512)
    return h2d.reshape(batch, seq, d_model)
```

```python
import math

import jax
import jax.numpy as jnp
from jax import lax
from jax.experimental import pallas as pl
from jax.experimental.pallas import tpu as pltpu

D_MODEL = 2048
HEAD_DIM = 128
N_HEADS_DIL = 8
D_DIL = N_HEADS_DIL * HEAD_DIM
DILATIONS = (16, 4, 1)
N_STEPS = 128
BLOCK = 128
N_GROUPS_SG = 4
SG_CHUNK = 128
D_SG = N_GROUPS_SG * SG_CHUNK
N_HEADS_MEM = 4
D_MEM_ATTN = N_HEADS_MEM * HEAD_DIM
N_BUCKETS = 32
MAX_DISTANCE = 2048
DEEPNORM_ALPHA = 2.0 ** 0.25
LN_EPS = 1e-5
ATTN_SCALE = HEAD_DIM ** -0.5
LOG2_E = math.log2(math.e)
MASK_VALUE = -1e30
N_CLASSES = 16
REGROUP_STRIDE = 4
SUBLANES = 8
ATTN_UNROLL = 16
COMBINE_ROWS = 64
PROJ_COL_TILES = 3
ROW_SPLIT = 2

VMEM_LIMIT_BYTES = 58 * 1024 * 1024

_NT_DIMS = (((1,), (1,)), ((), ()))


def _layer_norm(y, g, b):
    mu = jnp.mean(y, axis=-1, keepdims=True)
    d = y - mu
    var = jnp.mean(d * d, axis=-1, keepdims=True)
    return d * lax.rsqrt(var + LN_EPS) * g + b


def _compiler_params(semantics):
    return pltpu.CompilerParams(dimension_semantics=semantics, vmem_limit_bytes=VMEM_LIMIT_BYTES)


def _resident(block_shape, index_map):
    return pl.BlockSpec(block_shape, index_map, pipeline_mode=pl.Buffered(1))


def _proj_kernel(x_ref, w_ref, o_ref, xb_ref):
    @pl.when(pl.program_id(1) == 0)
    def _():
        xb_ref[...] = x_ref[...].astype(jnp.bfloat16)

    o_ref[...] = jnp.dot(xb_ref[...], w_ref[pl.program_id(1)], preferred_element_type=jnp.float32)


def _proj(x2d, w_tiles_bf16, bm):
    m, k = x2d.shape
    n_tiles, _, bn = w_tiles_bf16.shape
    return pl.pallas_call(
        _proj_kernel,
        grid=(m // bm, n_tiles),
        in_specs=[pl.BlockSpec((bm, k), lambda i, j: (i, 0)),
                  _resident(w_tiles_bf16.shape, lambda i, j: (0, 0, 0))],
        out_specs=pl.BlockSpec((bm, bn), lambda i, j: (i, j)),
        out_shape=jax.ShapeDtypeStruct((m, n_tiles * bn), jnp.float32),
        scratch_shapes=[pltpu.VMEM((bm, k), jnp.bfloat16)],
        compiler_params=_compiler_params(("parallel", "arbitrary")),
        name="proj",
    )(x2d, w_tiles_bf16)


def _kv_kernel(x_ref, w_ref, o_ref):
    o_ref[...] = jnp.dot(x_ref[...].astype(jnp.bfloat16), w_ref[...].astype(jnp.bfloat16),
                         preferred_element_type=jnp.float32).astype(o_ref.dtype)


def _kv_proj(mem2d, w, bn):
    m, k = mem2d.shape
    n = w.shape[1]
    return pl.pallas_call(
        _kv_kernel,
        grid=(n // bn,),
        in_specs=[pl.BlockSpec((m, k), lambda j: (0, 0)), pl.BlockSpec((k, bn), lambda j: (0, j))],
        out_specs=pl.BlockSpec((m, bn), lambda j: (0, j)),
        out_shape=jax.ShapeDtypeStruct((m, n), jnp.bfloat16),
        compiler_params=_compiler_params(("parallel",)),
        name="kv_proj",
    )(mem2d, w)


def _t5_bucket(dist):
    max_exact = N_BUCKETS // 2
    d = jnp.maximum(dist, 1).astype(jnp.float32)
    large = max_exact + (jnp.log(d / max_exact) / math.log(MAX_DISTANCE / max_exact)
                         * (N_BUCKETS - max_exact)).astype(jnp.int32)
    large = jnp.minimum(large, N_BUCKETS - 1)
    return jnp.where(dist < max_exact, dist, large)


def _log2(n):
    assert n & (n - 1) == 0
    return n.bit_length() - 1


def _class_slot(c_sub, dil, mcls):
    if dil == 1:
        return REGROUP_STRIDE * (mcls % REGROUP_STRIDE) + mcls // REGROUP_STRIDE
    if dil == REGROUP_STRIDE:
        return c_sub * REGROUP_STRIDE + mcls
    assert dil == N_CLASSES and mcls == 0
    lo = lax.bitwise_and(c_sub, REGROUP_STRIDE - 1)
    hi = lax.shift_right_logical(c_sub, _log2(REGROUP_STRIDE))
    return lo * REGROUP_STRIDE + hi


def _regroup(src_ref, tmp_ref, dst_ref, scale=None):
    seq = src_ref.shape[0]
    quarter = seq // REGROUP_STRIDE
    rows_per_class = seq // N_CLASSES
    for lo in range(REGROUP_STRIDE):
        rows = src_ref[pl.ds(lo, quarter, stride=REGROUP_STRIDE), :]
        tmp_ref[lo * quarter:(lo + 1) * quarter, :] = rows if scale is None else rows * scale
    for lo in range(REGROUP_STRIDE):
        for hi in range(REGROUP_STRIDE):
            slot = lo * REGROUP_STRIDE + hi
            dst_ref[slot * rows_per_class:(slot + 1) * rows_per_class, :] = (
                tmp_ref[pl.ds(lo * quarter + hi, rows_per_class, stride=REGROUP_STRIDE), :])


def _ungroup(read_rows, piece, tmp_ref, dst_ref):
    seq = dst_ref.shape[0]
    quarter = seq // REGROUP_STRIDE
    rows_per_class = seq // N_CLASSES
    for lo in range(REGROUP_STRIDE):
        for hi in range(REGROUP_STRIDE):
            slot = lo * REGROUP_STRIDE + hi
            for i0 in range(0, rows_per_class, piece):
                dst_rows = pl.ds(lo * quarter + hi + REGROUP_STRIDE * i0, piece, stride=REGROUP_STRIDE)
                tmp_ref[dst_rows, :] = read_rows(slot * rows_per_class + i0, piece)
    for lo in range(REGROUP_STRIDE):
        dst_ref[pl.ds(lo, quarter, stride=REGROUP_STRIDE), :] = tmp_ref[lo * quarter:(lo + 1) * quarter, :]


def _dil_attn_kernel(rb_ref, q_ref, k_ref, v_ref, side_ref, o_ref, side_out_ref,
                     qs_ref, ks_ref, vs_ref, tmp_ref, acc_ref, m_ref, l_ref, bias_ref):
    seq = q_ref.shape[0]
    rows_per_class = seq // N_CLASSES
    h = pl.program_id(0)

    side_out_ref[...] = side_ref[...].astype(jnp.bfloat16)

    _regroup(q_ref, tmp_ref, qs_ref, scale=ATTN_SCALE * LOG2_E)
    _regroup(k_ref, tmp_ref, ks_ref)
    _regroup(v_ref, tmp_ref, vs_ref)

    @pl.when(pl.program_id(1) == 0)
    def _():
        row = lax.broadcasted_iota(jnp.int32, (BLOCK, 2 * BLOCK), 0)
        col = lax.broadcasted_iota(jnp.int32, (BLOCK, 2 * BLOCK), 1)
        for p, dil in enumerate(DILATIONS):
            n_cls = N_CLASSES // dil
            q_rows = BLOCK // n_cls
            k_rows = 2 * BLOCK // n_cls
            q_sub = (row % q_rows) * n_cls + row // q_rows
            k_sub = (col % k_rows) * n_cls + col // k_rows
            for variant, shift in enumerate((N_STEPS, 0)):
                steps = q_sub + shift - k_sub
                valid = (steps >= 0) & (steps <= N_STEPS)
                bucket = _t5_bucket(jnp.maximum(steps, 0) * dil)
                bias = jnp.zeros((BLOCK, 2 * BLOCK), jnp.float32)
                for b in range(N_BUCKETS):
                    bias = jnp.where(bucket == b, rb_ref[b, h], bias)
                bias_ref[2 * p + variant] = jnp.where(valid, bias * LOG2_E, MASK_VALUE)

    for p, dil in enumerate(DILATIONS):
        n_cls = N_CLASSES // dil
        q_rows = BLOCK // n_cls
        k_rows = 2 * BLOCK // n_cls
        blocks_per_sub = seq // dil // BLOCK

        def blocks_body(it, carry, p=p, dil=dil, n_cls=n_cls, q_rows=q_rows, k_rows=k_rows,
                        blocks_per_sub=blocks_per_sub):
            staged = []
            for u in range(ATTN_UNROLL):
                blk = it * ATTN_UNROLL + u
                c_sub = lax.shift_right_logical(blk, _log2(blocks_per_sub))
                n = lax.bitwise_and(blk, blocks_per_sub - 1)
                first = jnp.where(n == 0, 1, 0)
                k_back = q_rows - first * q_rows
                q_offs, k_offs = [], []
                for mcls in range(n_cls):
                    base = _class_slot(c_sub, dil, mcls) * rows_per_class + n * q_rows
                    q_offs.append(pl.multiple_of(base, SUBLANES))
                    k_offs.append(pl.multiple_of(base - k_back, SUBLANES))
                qb = jnp.concatenate([qs_ref[pl.ds(o, q_rows), :] for o in q_offs], axis=0)
                kb = jnp.concatenate([ks_ref[pl.ds(o, k_rows), :] for o in k_offs], axis=0)
                s = lax.dot_general(qb.astype(jnp.bfloat16), kb.astype(jnp.bfloat16), _NT_DIMS,
                                    preferred_element_type=jnp.float32)
                staged.append((q_offs, k_offs, s + bias_ref[2 * p + first]))
            softmaxed = []
            for q_offs, k_offs, s in staged:
                m_blk = jnp.max(s, axis=-1, keepdims=True)
                e = jnp.exp2(s - m_blk)
                l_blk = jnp.sum(e, axis=-1, keepdims=True)
                softmaxed.append((q_offs, k_offs, m_blk, l_blk, e.astype(jnp.bfloat16)))
            for q_offs, k_offs, m_blk, l_blk, e in softmaxed:
                vb = jnp.concatenate([vs_ref[pl.ds(o, k_rows), :] for o in k_offs], axis=0)
                pv = jnp.dot(e, vb.astype(jnp.bfloat16), preferred_element_type=jnp.float32)
                for mcls in range(n_cls):
                    rows = pl.ds(q_offs[mcls], q_rows)
                    sl = slice(mcls * q_rows, (mcls + 1) * q_rows)
                    m_ref[p, rows, :] = jnp.broadcast_to(m_blk[sl], (q_rows, HEAD_DIM))
                    l_ref[p, rows, :] = jnp.broadcast_to(l_blk[sl], (q_rows, HEAD_DIM))
                    acc_ref[p, rows, :] = pv[sl]
            return carry

        lax.fori_loop(0, dil * blocks_per_sub // ATTN_UNROLL, blocks_body, 0)

    def combined(start, size):
        rows = slice(start, start + size)
        maxes = [m_ref[p, rows, :] for p in range(len(DILATIONS))]
        m_all = jnp.maximum(jnp.maximum(maxes[0], maxes[1]), maxes[2])
        den = jnp.zeros((size, HEAD_DIM), jnp.float32)
        num = jnp.zeros((size, HEAD_DIM), jnp.float32)
        for p in range(len(DILATIONS)):
            w = jnp.exp2(maxes[p] - m_all)
            den = den + w * l_ref[p, rows, :]
            num = num + w * acc_ref[p, rows, :]
        return num / den

    _ungroup(combined, COMBINE_ROWS, tmp_ref, o_ref)


def _dilated_attention(hcat, rel_bias, w_side, batch, seq):
    hcat3 = hcat.reshape(batch, seq, hcat.shape[-1])
    blk = (None, seq, HEAD_DIM)
    rows = pltpu.VMEM((seq, HEAD_DIM), jnp.float32)
    per_pattern = pltpu.VMEM((len(DILATIONS), seq, HEAD_DIM), jnp.float32)
    side_rows = w_side.shape[0] // (N_HEADS_DIL * batch)
    side_spec = pl.BlockSpec((side_rows, w_side.shape[1]), lambda h, b: (h * batch + b, 0))
    return pl.pallas_call(
        _dil_attn_kernel,
        grid=(N_HEADS_DIL, batch),
        in_specs=[pl.BlockSpec(memory_space=pltpu.SMEM),
                  pl.BlockSpec(blk, lambda h, b: (b, 0, h)),
                  pl.BlockSpec(blk, lambda h, b: (b, 0, N_HEADS_DIL + h)),
                  pl.BlockSpec(blk, lambda h, b: (b, 0, 2 * N_HEADS_DIL + h)),
                  side_spec],
        out_specs=[pl.BlockSpec(blk, lambda h, b: (b, 0, h)), side_spec],
        out_shape=[jax.ShapeDtypeStruct((batch, seq, D_DIL), jnp.float32),
                   jax.ShapeDtypeStruct(w_side.shape, jnp.bfloat16)],
        scratch_shapes=[rows] * 4 + [per_pattern] * 3
        + [pltpu.VMEM((2 * len(DILATIONS), BLOCK, 2 * BLOCK), jnp.float32)],
        compiler_params=_compiler_params(("arbitrary", "arbitrary")),
        name="dilated_attn",
    )(rel_bias, hcat3, hcat3, hcat3, w_side)


def _mix_kernel(x_ref, odil_ref, u_ref, v_ref, qm_ref, kv_ref, wsp_ref, bsp_ref, sgg_ref, sgb_ref,
                wout_ref, g1_ref, b1_ref, o_ref, ob_ref):
    tm = x_ref.shape[0]

    r_i = lax.broadcasted_iota(jnp.int32, (SG_CHUNK, SG_CHUNK), 0)
    c_i = lax.broadcasted_iota(jnp.int32, (SG_CHUNK, SG_CHUNK), 1)
    causal = r_i >= c_i
    w_sp = [jnp.where(causal, wsp_ref[g], 0.0).astype(jnp.bfloat16) for g in range(N_GROUPS_SG)]
    bsp = bsp_ref[...]
    u = jax.nn.gelu(u_ref[...])
    v = _layer_norm(jax.nn.gelu(v_ref[...]), sgg_ref[...], sgb_ref[...]).astype(jnp.bfloat16)
    sg_rows = []
    for ci in range(tm // SG_CHUNK):
        rs = slice(ci * SG_CHUNK, (ci + 1) * SG_CHUNK)
        cols = []
        for g in range(N_GROUPS_SG):
            cs = slice(g * SG_CHUNK, (g + 1) * SG_CHUNK)
            mixed = jnp.dot(w_sp[g], v[rs, cs], preferred_element_type=jnp.float32)
            cols.append((u[rs, cs] * (mixed + bsp[:, g:g + 1])).astype(jnp.bfloat16))
        sg_rows.append(jnp.concatenate(cols, axis=1))
    o_sg = jnp.concatenate(sg_rows, axis=0)

    qm = qm_ref[...].astype(jnp.bfloat16)
    mem_cols = []
    for hh in range(N_HEADS_MEM):
        cs = slice(hh * HEAD_DIM, (hh + 1) * HEAD_DIM)
        kh = kv_ref[:, hh * HEAD_DIM:(hh + 1) * HEAD_DIM]
        vh = kv_ref[:, D_MEM_ATTN + hh * HEAD_DIM:D_MEM_ATTN + (hh + 1) * HEAD_DIM]
        s = lax.dot_general(qm[:, cs], kh, _NT_DIMS,
                            preferred_element_type=jnp.float32) * ATTN_SCALE
        s_max = jnp.max(s, axis=-1, keepdims=True)
        e = jnp.exp(s - s_max)
        den = jnp.sum(e, axis=-1, keepdims=True)
        o = jnp.dot(e.astype(jnp.bfloat16), vh, preferred_element_type=jnp.float32) / den
        mem_cols.append(o.astype(jnp.bfloat16))
    mix_in = jnp.concatenate([odil_ref[...].astype(jnp.bfloat16), o_sg] + mem_cols, axis=1)
    mix = jnp.dot(mix_in, wout_ref[...], preferred_element_type=jnp.float32)
    x1 = _layer_norm(DEEPNORM_ALPHA * x_ref[...] + mix, g1_ref[...], b1_ref[...])
    o_ref[...] = x1
    ob_ref[...] = x1.astype(jnp.bfloat16)


def _mix(x2d, o_dil2d, hcat, kv, w_sp, b_sp_t, sg_g, sg_b, w_out, g1, b1, seq, tm):
    m, d_model = x2d.shape
    tiles_per_batch = seq // tm
    n_mem = kv.shape[1]
    u_blk = 3 * D_DIL // D_SG
    const = lambda i: (0, 0)
    return pl.pallas_call(
        _mix_kernel,
        grid=(m // tm,),
        in_specs=[pl.BlockSpec((tm, d_model), lambda i: (i, 0)),
                  pl.BlockSpec((tm, D_DIL), lambda i: (i, 0)),
                  pl.BlockSpec((tm, D_SG), lambda i: (i, u_blk)),
                  pl.BlockSpec((tm, D_SG), lambda i: (i, u_blk + 1)),
                  pl.BlockSpec((tm, D_MEM_ATTN), lambda i: (i, u_blk + 2)),
                  pl.BlockSpec((None, n_mem, 2 * D_MEM_ATTN), lambda i: (i // tiles_per_batch, 0, 0)),
                  _resident((N_GROUPS_SG, SG_CHUNK, SG_CHUNK), lambda i: (0, 0, 0)),
                  _resident((SG_CHUNK, N_GROUPS_SG), const),
                  _resident((1, D_SG), const),
                  _resident((1, D_SG), const),
                  _resident((d_model, d_model), const),
                  _resident((1, d_model), const),
                  _resident((1, d_model), const)],
        out_specs=[pl.BlockSpec((tm, d_model), lambda i: (i, 0)),
                   pl.BlockSpec((tm, d_model), lambda i: (i, 0))],
        out_shape=[jax.ShapeDtypeStruct((m, d_model), jnp.float32),
                   jax.ShapeDtypeStruct((m, d_model), jnp.bfloat16)],
        compiler_params=_compiler_params(("parallel",)),
        name="mix_ln1",
    )(x2d, o_dil2d, hcat, hcat, hcat, kv, w_sp, b_sp_t, sg_g, sg_b, w_out, g1, b1)


FFN_COL_SPLIT = 2


def _ffn_hidden_kernel(x_ref, wg_ref, wu_ref, side_ref, h_ref, side_out_ref, wgb_ref, wub_ref):
    @pl.when(pl.program_id(1) == 0)
    def _():
        wgb_ref[...] = wg_ref[...].astype(jnp.bfloat16)
        wub_ref[...] = wu_ref[...].astype(jnp.bfloat16)

    side_out_ref[...] = side_ref[...].astype(jnp.bfloat16)
    x = x_ref[...]
    bn = h_ref.shape[1]
    part = bn // FFN_COL_SPLIT
    for k in range(FFN_COL_SPLIT):
        cs = slice(k * part, (k + 1) * part)
        gate = jnp.dot(x, wgb_ref[:, cs], preferred_element_type=jnp.float32)
        up = jnp.dot(x, wub_ref[:, cs], preferred_element_type=jnp.float32)
        h_ref[:, cs] = (jax.nn.silu(gate) * up).astype(h_ref.dtype)


def _ffn_hidden(x1b, wg, wu, w_side, bm, bn):
    m, d_model = x1b.shape
    d_ff = wg.shape[1]
    grid = (d_ff // bn, m // bm)
    side_rows = w_side.shape[0] // (grid[0] * grid[1])
    side_spec = pl.BlockSpec((side_rows, w_side.shape[1]), lambda j, i: (j * grid[1] + i, 0))
    return pl.pallas_call(
        _ffn_hidden_kernel,
        grid=grid,
        in_specs=[pl.BlockSpec((bm, d_model), lambda j, i: (i, 0)),
                  pl.BlockSpec((d_model, bn), lambda j, i: (0, j)),
                  pl.BlockSpec((d_model, bn), lambda j, i: (0, j)),
                  side_spec],
        out_specs=[pl.BlockSpec((bm, bn), lambda j, i: (i, j)), side_spec],
        out_shape=[jax.ShapeDtypeStruct((m, d_ff), jnp.bfloat16),
                   jax.ShapeDtypeStruct(w_side.shape, jnp.bfloat16)],
        scratch_shapes=[pltpu.VMEM((d_model, bn), jnp.bfloat16),
                        pltpu.VMEM((d_model, bn), jnp.bfloat16)],
        compiler_params=_compiler_params(("parallel", "arbitrary")),
        name="ffn_hidden",
    )(x1b, wg, wu, w_side)


def _ffn_down_kernel(h_ref, x_ref, wd_ref, g2_ref, b2_ref, o_ref, f_ref):
    sub = h_ref.shape[0] // ROW_SPLIT
    row_parts = [slice(k * sub, (k + 1) * sub) for k in range(ROW_SPLIT)]
    for rows in row_parts:
        f_ref[rows, :] = jnp.dot(h_ref[rows, :], wd_ref[...], preferred_element_type=jnp.float32)
    for rows in row_parts:
        o_ref[rows, :] = _layer_norm(DEEPNORM_ALPHA * x_ref[rows, :] + f_ref[rows, :],
                                     g2_ref[...], b2_ref[...])


def _ffn_down(hid, x1, wd, g2, b2, tm):
    m, d_ff = hid.shape
    d_model = wd.shape[1]
    const = lambda i: (0, 0)
    return pl.pallas_call(
        _ffn_down_kernel,
        grid=(m // tm,),
        in_specs=[pl.BlockSpec((tm, d_ff), lambda i: (i, 0)),
                  pl.BlockSpec((tm, d_model), lambda i: (i, 0)),
                  _resident((d_ff, d_model), const),
                  _resident((1, d_model), const),
                  _resident((1, d_model), const)],
        out_specs=pl.BlockSpec((tm, d_model), lambda i: (i, 0)),
        out_shape=jax.ShapeDtypeStruct((m, d_model), jnp.float32),
        scratch_shapes=[pltpu.VMEM((tm, d_model), jnp.float32)],
        compiler_params=_compiler_params(("parallel",)),
        name="ffn_down_ln2",
    )(hid, x1, wd, g2, b2)


def kernel(x, mem, w_in, rel_bias, sg_ln_g, sg_ln_b, w_spatial, b_spatial, w_mem_kv, w_out,
           ln1_g, ln1_b, w_gate, w_up, w_down, ln2_g, ln2_b):
    batch, seq, d_model = x.shape
    n_mem = mem.shape[1]
    depth = w_in.shape[0]
    h2d = x.reshape(batch * seq, d_model)
    for l in range(depth):
        w_in_tiles = (w_in[l].astype(jnp.bfloat16)
                      .reshape(d_model, PROJ_COL_TILES, -1).transpose(1, 0, 2))
        hcat = _proj(h2d, w_in_tiles, bm=1024)
        kv = _kv_proj(mem.reshape(batch * n_mem, d_model), w_mem_kv[l], bn=D_MEM_ATTN)
        o_dil, w_out_b = _dilated_attention(hcat, rel_bias, w_out[l], batch, seq)
        x1, x1b = _mix(h2d, o_dil.reshape(batch * seq, D_DIL), hcat,
                       kv.reshape(batch, n_mem, 2 * D_MEM_ATTN), w_spatial[l], b_spatial[l].T,
                       sg_ln_g[l][None], sg_ln_b[l][None], w_out_b,
                       ln1_g[l][None], ln1_b[l][None], seq, tm=512)
        hid, w_down_b = _ffn_hidden(x1b, w_gate[l], w_up[l], w_down[l], bm=2048, bn=512)
        h2d = _ffn_down(hid, x1, w_down_b, ln2_g[l][None], ln2_b[l][None], tm=512)
    return h2d.reshape(batch, seq, d_model)
```

```python
import functools
import math

import jax
import jax.numpy as jnp
from jax import lax
from jax.experimental import pallas as pl
from jax.experimental.pallas import tpu as pltpu

D_MODEL = 2048
HEAD_DIM = 128
N_HEADS_DIL = 8
D_DIL = N_HEADS_DIL * HEAD_DIM
DILATIONS = (16, 4, 1)
N_STEPS = 128
BLOCK = 128
N_GROUPS_SG = 4
SG_CHUNK = 128
D_SG = N_GROUPS_SG * SG_CHUNK
N_HEADS_MEM = 4
D_MEM_ATTN = N_HEADS_MEM * HEAD_DIM
N_BUCKETS = 32
MAX_DISTANCE = 2048
DEEPNORM_ALPHA = 2.0 ** 0.25
LN_EPS = 1e-5
ATTN_SCALE = HEAD_DIM ** -0.5
LOG2_E = math.log2(math.e)
MASK_VALUE = -1e30
N_CLASSES = 16
REGROUP_STRIDE = 4
SUBLANES = 8
ATTN_UNROLL = 16
COMBINE_ROWS = 64
PROJ_BN = 768
ROW_SPLIT = 2

VMEM_LIMIT_BYTES = 58 * 1024 * 1024

_NT_DIMS = (((1,), (1,)), ((), ()))


def _layer_norm(y, g, b):
    mu = jnp.mean(y, axis=-1, keepdims=True)
    d = y - mu
    var = jnp.mean(d * d, axis=-1, keepdims=True)
    return d * lax.rsqrt(var + LN_EPS) * g + b


def _compiler_params(semantics):
    return pltpu.CompilerParams(dimension_semantics=semantics, vmem_limit_bytes=VMEM_LIMIT_BYTES)


def _resident(block_shape, index_map):
    return pl.BlockSpec(block_shape, index_map, pipeline_mode=pl.Buffered(1))


def _proj_kernel(x_ref, w_ref, qkv_ref, rest_ref, xb_ref, *, qkv_tiles):
    @pl.when(pl.program_id(1) == 0)
    def _():
        xb_ref[...] = x_ref[...].astype(jnp.bfloat16)

    bn = qkv_ref.shape[1]
    for jj in range(w_ref.shape[1] // bn):
        @pl.when(pl.program_id(1) == jj)
        def _(jj=jj):
            res = jnp.dot(xb_ref[...], w_ref[:, jj * bn:(jj + 1) * bn],
                          preferred_element_type=jnp.float32)
            if jj < qkv_tiles:
                qkv_ref[...] = res
            else:
                rest_ref[...] = res.astype(rest_ref.dtype)


def _proj(x2d, w_bf16, n_qkv, bm, bn):
    m, k = x2d.shape
    n = w_bf16.shape[1]
    qkv_tiles = n_qkv // bn
    return pl.pallas_call(
        functools.partial(_proj_kernel, qkv_tiles=qkv_tiles),
        grid=(m // bm, n // bn),
        in_specs=[pl.BlockSpec((bm, k), lambda i, j: (i, 0)),
                  _resident((k, n), lambda i, j: (0, 0))],
        out_specs=[pl.BlockSpec((bm, bn), lambda i, j: (i, jnp.minimum(j, qkv_tiles - 1))),
                   pl.BlockSpec((bm, bn), lambda i, j: (i, jnp.maximum(j - qkv_tiles, 0)))],
        out_shape=[jax.ShapeDtypeStruct((m, n_qkv), jnp.float32),
                   jax.ShapeDtypeStruct((m, n - n_qkv), jnp.bfloat16)],
        scratch_shapes=[pltpu.VMEM((bm, k), jnp.bfloat16)],
        compiler_params=_compiler_params(("parallel", "arbitrary")),
        name="proj",
    )(x2d, w_bf16)


def _kv_kernel(x_ref, w_ref, o_ref):
    o_ref[...] = jnp.dot(x_ref[...].astype(jnp.bfloat16), w_ref[...].astype(jnp.bfloat16),
                         preferred_element_type=jnp.float32).astype(o_ref.dtype)


def _kv_proj(mem2d, w, bn):
    m, k = mem2d.shape
    n = w.shape[1]
    return pl.pallas_call(
        _kv_kernel,
        grid=(n // bn,),
        in_specs=[pl.BlockSpec((m, k), lambda j: (0, 0)), pl.BlockSpec((k, bn), lambda j: (0, j))],
        out_specs=pl.BlockSpec((m, bn), lambda j: (0, j)),
        out_shape=jax.ShapeDtypeStruct((m, n), jnp.bfloat16),
        compiler_params=_compiler_params(("parallel",)),
        name="kv_proj",
    )(mem2d, w)


def _t5_bucket(dist):
    max_exact = N_BUCKETS // 2
    d = jnp.maximum(dist, 1).astype(jnp.float32)
    large = max_exact + (jnp.log(d / max_exact) / math.log(MAX_DISTANCE / max_exact)
                         * (N_BUCKETS - max_exact)).astype(jnp.int32)
    large = jnp.minimum(large, N_BUCKETS - 1)
    return jnp.where(dist < max_exact, dist, large)


def _log2(n):
    assert n & (n - 1) == 0
    return n.bit_length() - 1


def _class_slot(c_sub, dil, mcls):
    if dil == 1:
        return REGROUP_STRIDE * (mcls % REGROUP_STRIDE) + mcls // REGROUP_STRIDE
    if dil == REGROUP_STRIDE:
        return c_sub * REGROUP_STRIDE + mcls
    assert dil == N_CLASSES and mcls == 0
    lo = lax.bitwise_and(c_sub, REGROUP_STRIDE - 1)
    hi = lax.shift_right_logical(c_sub, _log2(REGROUP_STRIDE))
    return lo * REGROUP_STRIDE + hi


def _regroup(src_ref, tmp_ref, dst_ref, scale=None):
    seq = src_ref.shape[0]
    quarter = seq // REGROUP_STRIDE
    rows_per_class = seq // N_CLASSES
    for lo in range(REGROUP_STRIDE):
        rows = src_ref[pl.ds(lo, quarter, stride=REGROUP_STRIDE), :]
        tmp_ref[lo * quarter:(lo + 1) * quarter, :] = rows if scale is None else rows * scale
    for lo in range(REGROUP_STRIDE):
        for hi in range(REGROUP_STRIDE):
            slot = lo * REGROUP_STRIDE + hi
            dst_ref[slot * rows_per_class:(slot + 1) * rows_per_class, :] = (
                tmp_ref[pl.ds(lo * quarter + hi, rows_per_class, stride=REGROUP_STRIDE), :])


def _ungroup(read_rows, piece, tmp_ref, dst_ref):
    seq = dst_ref.shape[0]
    quarter = seq // REGROUP_STRIDE
    rows_per_class = seq // N_CLASSES
    for lo in range(REGROUP_STRIDE):
        for hi in range(REGROUP_STRIDE):
            slot = lo * REGROUP_STRIDE + hi
            for i0 in range(0, rows_per_class, piece):
                dst_rows = pl.ds(lo * quarter + hi + REGROUP_STRIDE * i0, piece, stride=REGROUP_STRIDE)
                tmp_ref[dst_rows, :] = read_rows(slot * rows_per_class + i0, piece)
    for lo in range(REGROUP_STRIDE):
        dst_ref[pl.ds(lo, quarter, stride=REGROUP_STRIDE), :] = tmp_ref[lo * quarter:(lo + 1) * quarter, :]


def _dil_attn_kernel(rb_ref, q_ref, k_ref, v_ref, side_ref, o_ref, side_out_ref,
                     qs_ref, ks_ref, vs_ref, tmp_ref, acc_ref, m_ref, l_ref, bias_ref):
    seq = q_ref.shape[0]
    rows_per_class = seq // N_CLASSES
    h = pl.program_id(0)

    side_out_ref[...] = side_ref[...].astype(jnp.bfloat16)

    _regroup(q_ref, tmp_ref, qs_ref, scale=ATTN_SCALE * LOG2_E)
    _regroup(k_ref, tmp_ref, ks_ref)
    _regroup(v_ref, tmp_ref, vs_ref)

    @pl.when(pl.program_id(1) == 0)
    def _():
        row = lax.broadcasted_iota(jnp.int32, (BLOCK, 2 * BLOCK), 0)
        col = lax.broadcasted_iota(jnp.int32, (BLOCK, 2 * BLOCK), 1)
        for p, dil in enumerate(DILATIONS):
            n_cls = N_CLASSES // dil
            q_rows = BLOCK // n_cls
            k_rows = 2 * BLOCK // n_cls
            q_sub = (row % q_rows) * n_cls + row // q_rows
            k_sub = (col % k_rows) * n_cls + col // k_rows
            for variant, shift in enumerate((N_STEPS, 0)):
                steps = q_sub + shift - k_sub
                valid = (steps >= 0) & (steps <= N_STEPS)
                bucket = _t5_bucket(jnp.maximum(steps, 0) * dil)
                bias = jnp.zeros((BLOCK, 2 * BLOCK), jnp.float32)
                for b in range(N_BUCKETS):
                    bias = jnp.where(bucket == b, rb_ref[b, h], bias)
                bias_ref[2 * p + variant] = jnp.where(valid, bias * LOG2_E, MASK_VALUE)

    for p, dil in enumerate(DILATIONS):
        n_cls = N_CLASSES // dil
        q_rows = BLOCK // n_cls
        k_rows = 2 * BLOCK // n_cls
        blocks_per_sub = seq // dil // BLOCK

        def blocks_body(it, carry, p=p, dil=dil, n_cls=n_cls, q_rows=q_rows, k_rows=k_rows,
                        blocks_per_sub=blocks_per_sub):
            staged = []
            for u in range(ATTN_UNROLL):
                blk = it * ATTN_UNROLL + u
                c_sub = lax.shift_right_logical(blk, _log2(blocks_per_sub))
                n = lax.bitwise_and(blk, blocks_per_sub - 1)
                first = jnp.where(n == 0, 1, 0)
                k_back = q_rows - first * q_rows
                q_offs, k_offs = [], []
                for mcls in range(n_cls):
                    base = _class_slot(c_sub, dil, mcls) * rows_per_class + n * q_rows
                    q_offs.append(pl.multiple_of(base, SUBLANES))
                    k_offs.append(pl.multiple_of(base - k_back, SUBLANES))
                qb = jnp.concatenate([qs_ref[pl.ds(o, q_rows), :] for o in q_offs], axis=0)
                kb = jnp.concatenate([ks_ref[pl.ds(o, k_rows), :] for o in k_offs], axis=0)
                s = lax.dot_general(qb.astype(jnp.bfloat16), kb.astype(jnp.bfloat16), _NT_DIMS,
                                    preferred_element_type=jnp.float32)
                staged.append((q_offs, k_offs, s + bias_ref[2 * p + first]))
            softmaxed = []
            for q_offs, k_offs, s in staged:
                m_blk = jnp.max(s, axis=-1, keepdims=True)
                e = jnp.exp2(s - m_blk)
                l_blk = jnp.sum(e, axis=-1, keepdims=True)
                softmaxed.append((q_offs, k_offs, m_blk, l_blk, e.astype(jnp.bfloat16)))
            for q_offs, k_offs, m_blk, l_blk, e in softmaxed:
                vb = jnp.concatenate([vs_ref[pl.ds(o, k_rows), :] for o in k_offs], axis=0)
                pv = jnp.dot(e, vb.astype(jnp.bfloat16), preferred_element_type=jnp.float32)
                for mcls in range(n_cls):
                    rows = pl.ds(q_offs[mcls], q_rows)
                    sl = slice(mcls * q_rows, (mcls + 1) * q_rows)
                    m_ref[p, rows, :] = jnp.broadcast_to(m_blk[sl], (q_rows, HEAD_DIM))
                    l_ref[p, rows, :] = jnp.broadcast_to(l_blk[sl], (q_rows, HEAD_DIM))
                    acc_ref[p, rows, :] = pv[sl]
            return carry

        lax.fori_loop(0, dil * blocks_per_sub // ATTN_UNROLL, blocks_body, 0)

    def combined(start, size):
        rows = slice(start, start + size)
        maxes = [m_ref[p, rows, :] for p in range(len(DILATIONS))]
        m_all = jnp.maximum(jnp.maximum(maxes[0], maxes[1]), maxes[2])
        den = jnp.zeros((size, HEAD_DIM), jnp.float32)
        num = jnp.zeros((size, HEAD_DIM), jnp.float32)
        for p in range(len(DILATIONS)):
            w = jnp.exp2(maxes[p] - m_all)
            den = den + w * l_ref[p, rows, :]
            num = num + w * acc_ref[p, rows, :]
        return num / den

    _ungroup(combined, COMBINE_ROWS, tmp_ref, o_ref)


def _dilated_attention(qkv, rel_bias, w_side, batch, seq):
    hcat3 = qkv.reshape(batch, seq, qkv.shape[-1])
    blk = (None, seq, HEAD_DIM)
    rows = pltpu.VMEM((seq, HEAD_DIM), jnp.float32)
    per_pattern = pltpu.VMEM((len(DILATIONS), seq, HEAD_DIM), jnp.float32)
    side_rows = w_side.shape[0] // (N_HEADS_DIL * batch)
    side_spec = pl.BlockSpec((side_rows, w_side.shape[1]), lambda h, b: (h * batch + b, 0))
    return pl.pallas_call(
        _dil_attn_kernel,
        grid=(N_HEADS_DIL, batch),
        in_specs=[pl.BlockSpec(memory_space=pltpu.SMEM),
                  pl.BlockSpec(blk, lambda h, b: (b, 0, h)),
                  pl.BlockSpec(blk, lambda h, b: (b, 0, N_HEADS_DIL + h)),
                  pl.BlockSpec(blk, lambda h, b: (b, 0, 2 * N_HEADS_DIL + h)),
                  side_spec],
        out_specs=[pl.BlockSpec(blk, lambda h, b: (b, 0, h)), side_spec],
        out_shape=[jax.ShapeDtypeStruct((batch, seq, D_DIL), jnp.float32),
                   jax.ShapeDtypeStruct(w_side.shape, jnp.bfloat16)],
        scratch_shapes=[rows] * 4 + [per_pattern] * 3
        + [pltpu.VMEM((2 * len(DILATIONS), BLOCK, 2 * BLOCK), jnp.float32)],
        compiler_params=_compiler_params(("arbitrary", "arbitrary")),
        name="dilated_attn",
    )(rel_bias, hcat3, hcat3, hcat3, w_side)


def _mix_kernel(x_ref, odil_ref, u_ref, v_ref, qm_ref, kv_ref, wsp_ref, bsp_ref, sgg_ref, sgb_ref,
                wout_ref, g1_ref, b1_ref, o_ref, ob_ref):
    tm = x_ref.shape[0]

    r_i = lax.broadcasted_iota(jnp.int32, (SG_CHUNK, SG_CHUNK), 0)
    c_i = lax.broadcasted_iota(jnp.int32, (SG_CHUNK, SG_CHUNK), 1)
    causal = r_i >= c_i
    w_sp = [jnp.where(causal, wsp_ref[g], 0.0).astype(jnp.bfloat16) for g in range(N_GROUPS_SG)]
    bsp = bsp_ref[...]
    u = jax.nn.gelu(u_ref[...].astype(jnp.float32))
    v = _layer_norm(jax.nn.gelu(v_ref[...].astype(jnp.float32)), sgg_ref[...], sgb_ref[...])
    v = v.astype(jnp.bfloat16)
    sg_rows = []
    for ci in range(tm // SG_CHUNK):
        rs = slice(ci * SG_CHUNK, (ci + 1) * SG_CHUNK)
        cols = []
        for g in range(N_GROUPS_SG):
            cs = slice(g * SG_CHUNK, (g + 1) * SG_CHUNK)
            mixed = jnp.dot(w_sp[g], v[rs, cs], preferred_element_type=jnp.float32)
            cols.append((u[rs, cs] * (mixed + bsp[:, g:g + 1])).astype(jnp.bfloat16))
        sg_rows.append(jnp.concatenate(cols, axis=1))
    o_sg = jnp.concatenate(sg_rows, axis=0)

    qm = qm_ref[...]
    mem_cols = []
    for hh in range(N_HEADS_MEM):
        cs = slice(hh * HEAD_DIM, (hh + 1) * HEAD_DIM)
        kh = kv_ref[:, hh * HEAD_DIM:(hh + 1) * HEAD_DIM]
        vh = kv_ref[:, D_MEM_ATTN + hh * HEAD_DIM:D_MEM_ATTN + (hh + 1) * HEAD_DIM]
        s = lax.dot_general(qm[:, cs], kh, _NT_DIMS,
                            preferred_element_type=jnp.float32) * ATTN_SCALE
        s_max = jnp.max(s, axis=-1, keepdims=True)
        e = jnp.exp(s - s_max)
        den = jnp.sum(e, axis=-1, keepdims=True)
        o = jnp.dot(e.astype(jnp.bfloat16), vh, preferred_element_type=jnp.float32) / den
        mem_cols.append(o.astype(jnp.bfloat16))
    mix_in = jnp.concatenate([odil_ref[...].astype(jnp.bfloat16), o_sg] + mem_cols, axis=1)
    mix = jnp.dot(mix_in, wout_ref[...], preferred_element_type=jnp.float32)
    x1 = _layer_norm(DEEPNORM_ALPHA * x_ref[...] + mix, g1_ref[...], b1_ref[...])
    o_ref[...] = x1
    ob_ref[...] = x1.astype(jnp.bfloat16)


def _mix(x2d, o_dil2d, rest, kv, w_sp, b_sp_t, sg_g, sg_b, w_out, g1, b1, seq, tm):
    m, d_model = x2d.shape
    tiles_per_batch = seq // tm
    n_mem = kv.shape[1]
    const = lambda i: (0, 0)
    return pl.pallas_call(
        _mix_kernel,
        grid=(m // tm,),
        in_specs=[pl.BlockSpec((tm, d_model), lambda i: (i, 0)),
                  pl.BlockSpec((tm, D_DIL), lambda i: (i, 0)),
                  pl.BlockSpec((tm, D_SG), lambda i: (i, 0)),
                  pl.BlockSpec((tm, D_SG), lambda i: (i, 1)),
                  pl.BlockSpec((tm, D_MEM_ATTN), lambda i: (i, 2)),
                  pl.BlockSpec((None, n_mem, 2 * D_MEM_ATTN), lambda i: (i // tiles_per_batch, 0, 0)),
                  _resident((N_GROUPS_SG, SG_CHUNK, SG_CHUNK), lambda i: (0, 0, 0)),
                  _resident((SG_CHUNK, N_GROUPS_SG), const),
                  _resident((1, D_SG), const),
                  _resident((1, D_SG), const),
                  _resident((d_model, d_model), const),
                  _resident((1, d_model), const),
                  _resident((1, d_model), const)],
        out_specs=[pl.BlockSpec((tm, d_model), lambda i: (i, 0)),
                   pl.BlockSpec((tm, d_model), lambda i: (i, 0))],
        out_shape=[jax.ShapeDtypeStruct((m, d_model), jnp.float32),
                   jax.ShapeDtypeStruct((m, d_model), jnp.bfloat16)],
        compiler_params=_compiler_params(("parallel",)),
        name="mix_ln1",
    )(x2d, o_dil2d, rest, rest, rest, kv, w_sp, b_sp_t, sg_g, sg_b, w_out, g1, b1)


FFN_COL_SPLIT = 2


def _ffn_hidden_kernel(x_ref, wg_ref, wu_ref, side_ref, h_ref, side_out_ref, wgb_ref, wub_ref):
    @pl.when(pl.program_id(1) == 0)
    def _():
        wgb_ref[...] = wg_ref[...].astype(jnp.bfloat16)
        wub_ref[...] = wu_ref[...].astype(jnp.bfloat16)

    side_out_ref[...] = side_ref[...].astype(jnp.bfloat16)
    x = x_ref[...]
    bn = h_ref.shape[1]
    part = bn // FFN_COL_SPLIT
    for k in range(FFN_COL_SPLIT):
        cs = slice(k * part, (k + 1) * part)
        gate = jnp.dot(x, wgb_ref[:, cs], preferred_element_type=jnp.float32)
        up = jnp.dot(x, wub_ref[:, cs], preferred_element_type=jnp.float32)
        h_ref[:, cs] = (jax.nn.silu(gate) * up).astype(h_ref.dtype)


def _ffn_hidden(x1b, wg, wu, w_side, bm, bn):
    m, d_model = x1b.shape
    d_ff = wg.shape[1]
    grid = (d_ff // bn, m // bm)
    side_rows = w_side.shape[0] // (grid[0] * grid[1])
    side_spec = pl.BlockSpec((side_rows, w_side.shape[1]), lambda j, i: (j * grid[1] + i, 0))
    return pl.pallas_call(
        _ffn_hidden_kernel,
        grid=grid,
        in_specs=[pl.BlockSpec((bm, d_model), lambda j, i: (i, 0)),
                  pl.BlockSpec((d_model, bn), lambda j, i: (0, j)),
                  pl.BlockSpec((d_model, bn), lambda j, i: (0, j)),
                  side_spec],
        out_specs=[pl.BlockSpec((bm, bn), lambda j, i: (i, j)), side_spec],
        out_shape=[jax.ShapeDtypeStruct((m, d_ff), jnp.bfloat16),
                   jax.ShapeDtypeStruct(w_side.shape, jnp.bfloat16)],
        scratch_shapes=[pltpu.VMEM((d_model, bn), jnp.bfloat16),
                        pltpu.VMEM((d_model, bn), jnp.bfloat16)],
        compiler_params=_compiler_params(("parallel", "arbitrary")),
        name="ffn_hidden",
    )(x1b, wg, wu, w_side)


def _ffn_down_kernel(h_ref, x_ref, wd_ref, g2_ref, b2_ref, o_ref, f_ref):
    sub = h_ref.shape[0] // ROW_SPLIT
    row_parts = [slice(k * sub, (k + 1) * sub) for k in range(ROW_SPLIT)]
    for rows in row_parts:
        f_ref[rows, :] = jnp.dot(h_ref[rows, :], wd_ref[...], preferred_element_type=jnp.float32)
    for rows in row_parts:
        o_ref[rows, :] = _layer_norm(DEEPNORM_ALPHA * x_ref[rows, :] + f_ref[rows, :],
                                     g2_ref[...], b2_ref[...])


def _ffn_down(hid, x1, wd, g2, b2, tm):
    m, d_ff = hid.shape
    d_model = wd.shape[1]
    const = lambda i: (0, 0)
    return pl.pallas_call(
        _ffn_down_kernel,
        grid=(m // tm,),
        in_specs=[pl.BlockSpec((tm, d_ff), lambda i: (i, 0)),
                  pl.BlockSpec((tm, d_model), lambda i: (i, 0)),
                  _resident((d_ff, d_model), const),
                  _resident((1, d_model), const),
                  _resident((1, d_model), const)],
        out_specs=pl.BlockSpec((tm, d_model), lambda i: (i, 0)),
        out_shape=jax.ShapeDtypeStruct((m, d_model), jnp.float32),
        scratch_shapes=[pltpu.VMEM((tm, d_model), jnp.float32)],
        compiler_params=_compiler_params(("parallel",)),
        name="ffn_down_ln2",
    )(hid, x1, wd, g2, b2)


def kernel(x, mem, w_in, rel_bias, sg_ln_g, sg_ln_b, w_spatial, b_spatial, w_mem_kv, w_out,
           ln1_g, ln1_b, w_gate, w_up, w_down, ln2_g, ln2_b):
    batch, seq, d_model = x.shape
    n_mem = mem.shape[1]
    depth = w_in.shape[0]
    h2d = x.reshape(batch * seq, d_model)
    for l in range(depth):
        qkv, rest = _proj(h2d, w_in[l].astype(jnp.bfloat16), 3 * D_DIL, bm=1024, bn=PROJ_BN)
        kv = _kv_proj(mem.reshape(batch * n_mem, d_model), w_mem_kv[l], bn=D_MEM_ATTN)
        o_dil, w_out_b = _dilated_attention(qkv, rel_bias, w_out[l], batch, seq)
        x1, x1b = _mix(h2d, o_dil.reshape(batch * seq, D_DIL), rest,
                       kv.reshape(batch, n_mem, 2 * D_MEM_ATTN), w_spatial[l], b_spatial[l].T,
                       sg_ln_g[l][None], sg_ln_b[l][None], w_out_b,
                       ln1_g[l][None], ln1_b[l][None], seq, tm=512)
        hid, w_down_b = _ffn_hidden(x1b, w_gate[l], w_up[l], w_down[l], bm=2048, bn=512)
        h2d = _ffn_down(hid, x1, w_down_b, ln2_g[l][None], ln2_b[l][None], tm=512)
    return h2d.reshape(batch, seq, d_model)
```

```python
import math

import jax
import jax.numpy as jnp
import numpy as np
from jax import lax
from jax.experimental import pallas as pl
from jax.experimental.pallas import tpu as pltpu

D_MODEL = 2048
HEAD_DIM = 128
N_HEADS_DIL = 8
D_DIL = N_HEADS_DIL * HEAD_DIM
DILATIONS = (16, 4, 1)
N_STEPS = 128
BLOCK = 128
N_GROUPS_SG = 4
SG_CHUNK = 128
D_SG = N_GROUPS_SG * SG_CHUNK
N_HEADS_MEM = 4
D_MEM_ATTN = N_HEADS_MEM * HEAD_DIM
N_BUCKETS = 32
MAX_DISTANCE = 2048
DEEPNORM_ALPHA = 2.0 ** 0.25
LN_EPS = 1e-5
ATTN_SCALE = HEAD_DIM ** -0.5
LOG2_E = math.log2(math.e)
MASK_VALUE = -1e30
N_CLASSES = 16
REGROUP_STRIDE = 4
SUBLANES = 8
ATTN_UNROLL = 16
COMBINE_ROWS = 64
PROJ_BN = 1536
ROW_SPLIT = 2

VMEM_LIMIT_BYTES = 58 * 1024 * 1024

_NT_DIMS = (((1,), (1,)), ((), ()))


def _layer_norm(y, g, b):
    mu = jnp.mean(y, axis=-1, keepdims=True)
    d = y - mu
    var = jnp.mean(d * d, axis=-1, keepdims=True)
    return d * lax.rsqrt(var + LN_EPS) * g + b


def _compiler_params(semantics):
    return pltpu.CompilerParams(dimension_semantics=semantics, vmem_limit_bytes=VMEM_LIMIT_BYTES)


def _resident(block_shape, index_map):
    return pl.BlockSpec(block_shape, index_map, pipeline_mode=pl.Buffered(1))


def _proj_kernel(x_ref, w_ref, o_ref, xb_ref):
    @pl.when(pl.program_id(1) == 0)
    def _():
        xb_ref[...] = x_ref[...].astype(jnp.bfloat16)

    bn = o_ref.shape[1]
    for jj in range(w_ref.shape[1] // bn):
        @pl.when(pl.program_id(1) == jj)
        def _(jj=jj):
            o_ref[...] = jnp.dot(xb_ref[...], w_ref[:, jj * bn:(jj + 1) * bn],
                                 preferred_element_type=jnp.float32)


def _proj(x2d, w_bf16, bm, bn):
    m, k = x2d.shape
    n = w_bf16.shape[1]
    return pl.pallas_call(
        _proj_kernel,
        grid=(m // bm, n // bn),
        in_specs=[pl.BlockSpec((bm, k), lambda i, j: (i, 0)),
                  _resident((k, n), lambda i, j: (0, 0))],
        out_specs=pl.BlockSpec((bm, bn), lambda i, j: (i, j)),
        out_shape=jax.ShapeDtypeStruct((m, n), jnp.float32),
        scratch_shapes=[pltpu.VMEM((bm, k), jnp.bfloat16)],
        compiler_params=_compiler_params(("parallel", "arbitrary")),
        name="proj",
    )(x2d, w_bf16)


def _kv_kernel(x_ref, w_ref, o_ref):
    o_ref[...] = jnp.dot(x_ref[...].astype(jnp.bfloat16), w_ref[...].astype(jnp.bfloat16),
                         preferred_element_type=jnp.float32).astype(o_ref.dtype)


def _kv_proj(mem2d, w, bn):
    m, k = mem2d.shape
    n = w.shape[1]
    return pl.pallas_call(
        _kv_kernel,
        grid=(n // bn,),
        in_specs=[pl.BlockSpec((m, k), lambda j: (0, 0)), pl.BlockSpec((k, bn), lambda j: (0, j))],
        out_specs=pl.BlockSpec((m, bn), lambda j: (0, j)),
        out_shape=jax.ShapeDtypeStruct((m, n), jnp.bfloat16),
        compiler_params=_compiler_params(("parallel",)),
        name="kv_proj",
    )(mem2d, w)


def _t5_bucket_starts():
    max_exact = N_BUCKETS // 2
    dist = np.arange(MAX_DISTANCE + 1)
    d = np.maximum(dist, 1).astype(np.float32)
    large = max_exact + (np.log(d / np.float32(max_exact)) / np.float32(math.log(MAX_DISTANCE / max_exact))
                         * np.float32(N_BUCKETS - max_exact)).astype(np.int32)
    bucket = np.where(dist < max_exact, dist, np.minimum(large, N_BUCKETS - 1))
    assert (np.diff(bucket) >= 0).all() and bucket[-1] == N_BUCKETS - 1
    return tuple(int(np.argmax(bucket >= b)) for b in range(N_BUCKETS))


_T5_BUCKET_STARTS = _t5_bucket_starts()


def _log2(n):
    assert n & (n - 1) == 0
    return n.bit_length() - 1


def _class_slot(c_sub, dil, mcls):
    if dil == 1:
        return REGROUP_STRIDE * (mcls % REGROUP_STRIDE) + mcls // REGROUP_STRIDE
    if dil == REGROUP_STRIDE:
        return c_sub * REGROUP_STRIDE + mcls
    assert dil == N_CLASSES and mcls == 0
    lo = lax.bitwise_and(c_sub, REGROUP_STRIDE - 1)
    hi = lax.shift_right_logical(c_sub, _log2(REGROUP_STRIDE))
    return lo * REGROUP_STRIDE + hi


def _regroup(src_ref, tmp_ref, dst_ref, scale=None):
    seq = src_ref.shape[0]
    quarter = seq // REGROUP_STRIDE
    rows_per_class = seq // N_CLASSES
    for lo in range(REGROUP_STRIDE):
        rows = src_ref[pl.ds(lo, quarter, stride=REGROUP_STRIDE), :]
        tmp_ref[lo * quarter:(lo + 1) * quarter, :] = rows if scale is None else rows * scale
    for lo in range(REGROUP_STRIDE):
        for hi in range(REGROUP_STRIDE):
            slot = lo * REGROUP_STRIDE + hi
            dst_ref[slot * rows_per_class:(slot + 1) * rows_per_class, :] = (
                tmp_ref[pl.ds(lo * quarter + hi, rows_per_class, stride=REGROUP_STRIDE), :])


def _ungroup(read_rows, piece, tmp_ref, dst_ref):
    seq = dst_ref.shape[0]
    quarter = seq // REGROUP_STRIDE
    rows_per_class = seq // N_CLASSES
    for lo in range(REGROUP_STRIDE):
        for hi in range(REGROUP_STRIDE):
            slot = lo * REGROUP_STRIDE + hi
            for i0 in range(0, rows_per_class, piece):
                dst_rows = pl.ds(lo * quarter + hi + REGROUP_STRIDE * i0, piece, stride=REGROUP_STRIDE)
                tmp_ref[dst_rows, :] = read_rows(slot * rows_per_class + i0, piece)
    for lo in range(REGROUP_STRIDE):
        dst_ref[pl.ds(lo, quarter, stride=REGROUP_STRIDE), :] = tmp_ref[lo * quarter:(lo + 1) * quarter, :]


def _dil_attn_kernel(rb_ref, q_ref, k_ref, v_ref, side_ref, o_ref, side_out_ref,
                     qs_ref, ks_ref, vs_ref, tmp_ref, acc_ref, m_ref, l_ref, bias_ref):
    seq = q_ref.shape[0]
    rows_per_class = seq // N_CLASSES
    h = pl.program_id(0)

    side_out_ref[...] = side_ref[...].astype(jnp.bfloat16)

    _regroup(q_ref, tmp_ref, qs_ref, scale=ATTN_SCALE * LOG2_E)
    _regroup(k_ref, tmp_ref, ks_ref)
    _regroup(v_ref, tmp_ref, vs_ref)

    @pl.when(pl.program_id(1) == 0)
    def _():
        row = lax.broadcasted_iota(jnp.int32, (BLOCK, 2 * BLOCK), 0)
        col = lax.broadcasted_iota(jnp.int32, (BLOCK, 2 * BLOCK), 1)
        for p, dil in enumerate(DILATIONS):
            n_cls = N_CLASSES // dil
            q_rows = BLOCK // n_cls
            k_rows = 2 * BLOCK // n_cls
            q_sub = (row % q_rows) * n_cls + row // q_rows
            k_sub = (col % k_rows) * n_cls + col // k_rows
            for variant, shift in enumerate((N_STEPS, 0)):
                steps = q_sub + shift - k_sub
                valid = (steps >= 0) & (steps <= N_STEPS)
                dist = jnp.maximum(steps, 0) * dil
                bias = jnp.full((BLOCK, 2 * BLOCK), rb_ref[0, h], jnp.float32)
                for b in range(1, N_BUCKETS):
                    if _T5_BUCKET_STARTS[b] <= N_STEPS * dil:
                        bias = jnp.where(dist >= _T5_BUCKET_STARTS[b], rb_ref[b, h], bias)
                bias_ref[2 * p + variant] = jnp.where(valid, bias * LOG2_E, MASK_VALUE)

    for p, dil in enumerate(DILATIONS):
        n_cls = N_CLASSES // dil
        q_rows = BLOCK // n_cls
        k_rows = 2 * BLOCK // n_cls
        blocks_per_sub = seq // dil // BLOCK

        def blocks_body(it, carry, p=p, dil=dil, n_cls=n_cls, q_rows=q_rows, k_rows=k_rows,
                        blocks_per_sub=blocks_per_sub):
            staged = []
            for u in range(ATTN_UNROLL):
                blk = it * ATTN_UNROLL + u
                c_sub = lax.shift_right_logical(blk, _log2(blocks_per_sub))
                n = lax.bitwise_and(blk, blocks_per_sub - 1)
                first = jnp.where(n == 0, 1, 0)
                k_back = q_rows - first * q_rows
                q_offs, k_offs = [], []
                for mcls in range(n_cls):
                    base = _class_slot(c_sub, dil, mcls) * rows_per_class + n * q_rows
                    q_offs.append(pl.multiple_of(base, SUBLANES))
                    k_offs.append(pl.multiple_of(base - k_back, SUBLANES))
                qb = jnp.concatenate([qs_ref[pl.ds(o, q_rows), :] for o in q_offs], axis=0)
                kb = jnp.concatenate([ks_ref[pl.ds(o, k_rows), :] for o in k_offs], axis=0)
                s = lax.dot_general(qb.astype(jnp.bfloat16), kb.astype(jnp.bfloat16), _NT_DIMS,
                                    preferred_element_type=jnp.float32)
                staged.append((q_offs, k_offs, s + bias_ref[2 * p + first]))
            softmaxed = []
            for q_offs, k_offs, s in staged:
                m_blk = jnp.max(s, axis=-1, keepdims=True)
                e = jnp.exp2(s - m_blk)
                l_blk = jnp.sum(e, axis=-1, keepdims=True)
                softmaxed.append((q_offs, k_offs, m_blk, l_blk, e.astype(jnp.bfloat16)))
            for q_offs, k_offs, m_blk, l_blk, e in softmaxed:
                vb = jnp.concatenate([vs_ref[pl.ds(o, k_rows), :] for o in k_offs], axis=0)
                pv = jnp.dot(e, vb.astype(jnp.bfloat16), preferred_element_type=jnp.float32)
                for mcls in range(n_cls):
                    rows = pl.ds(q_offs[mcls], q_rows)
                    sl = slice(mcls * q_rows, (mcls + 1) * q_rows)
                    m_ref[p, rows, :] = jnp.broadcast_to(m_blk[sl], (q_rows, HEAD_DIM))
                    l_ref[p, rows, :] = jnp.broadcast_to(l_blk[sl], (q_rows, HEAD_DIM))
                    acc_ref[p, rows, :] = pv[sl]
            return carry

        lax.fori_loop(0, dil * blocks_per_sub // ATTN_UNROLL, blocks_body, 0)

    def combined(start, size):
        rows = slice(start, start + size)
        maxes = [m_ref[p, rows, :] for p in range(len(DILATIONS))]
        m_all = jnp.maximum(jnp.maximum(maxes[0], maxes[1]), maxes[2])
        den = jnp.zeros((size, HEAD_DIM), jnp.float32)
        num = jnp.zeros((size, HEAD_DIM), jnp.float32)
        for p in range(len(DILATIONS)):
            w = jnp.exp2(maxes[p] - m_all)
            den = den + w * l_ref[p, rows, :]
            num = num + w * acc_ref[p, rows, :]
        return num / den

    _ungroup(combined, COMBINE_ROWS, tmp_ref, o_ref)


def _dilated_attention(hcat, rel_bias, w_side, batch, seq):
    hcat3 = hcat.reshape(batch, seq, hcat.shape[-1])
    blk = (None, seq, HEAD_DIM)
    rows = pltpu.VMEM((seq, HEAD_DIM), jnp.float32)
    per_pattern = pltpu.VMEM((len(DILATIONS), seq, HEAD_DIM), jnp.float32)
    side_rows = w_side.shape[0] // (N_HEADS_DIL * batch)
    side_spec = pl.BlockSpec((side_rows, w_side.shape[1]), lambda h, b: (h * batch + b, 0))
    return pl.pallas_call(
        _dil_attn_kernel,
        grid=(N_HEADS_DIL, batch),
        in_specs=[pl.BlockSpec(memory_space=pltpu.SMEM),
                  pl.BlockSpec(blk, lambda h, b: (b, 0, h)),
                  pl.BlockSpec(blk, lambda h, b: (b, 0, N_HEADS_DIL + h)),
                  pl.BlockSpec(blk, lambda h, b: (b, 0, 2 * N_HEADS_DIL + h)),
                  side_spec],
        out_specs=[pl.BlockSpec(blk, lambda h, b: (b, 0, h)), side_spec],
        out_shape=[jax.ShapeDtypeStruct((batch, seq, D_DIL), jnp.float32),
                   jax.ShapeDtypeStruct(w_side.shape, jnp.bfloat16)],
        scratch_shapes=[rows] * 4 + [per_pattern] * 3
        + [pltpu.VMEM((2 * len(DILATIONS), BLOCK, 2 * BLOCK), jnp.float32)],
        compiler_params=_compiler_params(("arbitrary", "arbitrary")),
        name="dilated_attn",
    )(rel_bias, hcat3, hcat3, hcat3, w_side)


def _mix_kernel(x_ref, odil_ref, u_ref, v_ref, qm_ref, kv_ref, wsp_ref, bsp_ref, sgg_ref, sgb_ref,
                wout_ref, g1_ref, b1_ref, o_ref, ob_ref):
    tm = x_ref.shape[0]

    r_i = lax.broadcasted_iota(jnp.int32, (SG_CHUNK, SG_CHUNK), 0)
    c_i = lax.broadcasted_iota(jnp.int32, (SG_CHUNK, SG_CHUNK), 1)
    causal = r_i >= c_i
    w_sp = [jnp.where(causal, wsp_ref[g], 0.0).astype(jnp.bfloat16) for g in range(N_GROUPS_SG)]
    bsp = bsp_ref[...]
    u = jax.nn.gelu(u_ref[...])
    v = _layer_norm(jax.nn.gelu(v_ref[...]), sgg_ref[...], sgb_ref[...]).astype(jnp.bfloat16)
    sg_rows = []
    for ci in range(tm // SG_CHUNK):
        rs = slice(ci * SG_CHUNK, (ci + 1) * SG_CHUNK)
        cols = []
        for g in range(N_GROUPS_SG):
            cs = slice(g * SG_CHUNK, (g + 1) * SG_CHUNK)
            mixed = jnp.dot(w_sp[g], v[rs, cs], preferred_element_type=jnp.float32)
            cols.append((u[rs, cs] * (mixed + bsp[:, g:g + 1])).astype(jnp.bfloat16))
        sg_rows.append(jnp.concatenate(cols, axis=1))
    o_sg = jnp.concatenate(sg_rows, axis=0)

    qm = qm_ref[...].astype(jnp.bfloat16)
    mem_cols = []
    for hh in range(N_HEADS_MEM):
        cs = slice(hh * HEAD_DIM, (hh + 1) * HEAD_DIM)
        kh = kv_ref[:, hh * HEAD_DIM:(hh + 1) * HEAD_DIM]
        vh = kv_ref[:, D_MEM_ATTN + hh * HEAD_DIM:D_MEM_ATTN + (hh + 1) * HEAD_DIM]
        s = lax.dot_general(qm[:, cs], kh, _NT_DIMS,
                            preferred_element_type=jnp.float32) * ATTN_SCALE
        s_max = jnp.max(s, axis=-1, keepdims=True)
        e = jnp.exp(s - s_max)
        den = jnp.sum(e, axis=-1, keepdims=True)
        o = jnp.dot(e.astype(jnp.bfloat16), vh, preferred_element_type=jnp.float32) / den
        mem_cols.append(o.astype(jnp.bfloat16))
    mix_in = jnp.concatenate([odil_ref[...].astype(jnp.bfloat16), o_sg] + mem_cols, axis=1)
    mix = jnp.dot(mix_in, wout_ref[...], preferred_element_type=jnp.float32)
    x1 = _layer_norm(DEEPNORM_ALPHA * x_ref[...] + mix, g1_ref[...], b1_ref[...])
    o_ref[...] = x1
    ob_ref[...] = x1.astype(jnp.bfloat16)


def _mix(x2d, o_dil2d, hcat, kv, w_sp, b_sp_t, sg_g, sg_b, w_out, g1, b1, seq, tm):
    m, d_model = x2d.shape
    tiles_per_batch = seq // tm
    n_mem = kv.shape[1]
    u_blk = 3 * D_DIL // D_SG
    const = lambda i: (0, 0)
    return pl.pallas_call(
        _mix_kernel,
        grid=(m // tm,),
        in_specs=[pl.BlockSpec((tm, d_model), lambda i: (i, 0)),
                  pl.BlockSpec((tm, D_DIL), lambda i: (i, 0)),
                  pl.BlockSpec((tm, D_SG), lambda i: (i, u_blk)),
                  pl.BlockSpec((tm, D_SG), lambda i: (i, u_blk + 1)),
                  pl.BlockSpec((tm, D_MEM_ATTN), lambda i: (i, u_blk + 2)),
                  pl.BlockSpec((None, n_mem, 2 * D_MEM_ATTN), lambda i: (i // tiles_per_batch, 0, 0)),
                  _resident((N_GROUPS_SG, SG_CHUNK, SG_CHUNK), lambda i: (0, 0, 0)),
                  _resident((SG_CHUNK, N_GROUPS_SG), const),
                  _resident((1, D_SG), const),
                  _resident((1, D_SG), const),
                  _resident((d_model, d_model), const),
                  _resident((1, d_model), const),
                  _resident((1, d_model), const)],
        out_specs=[pl.BlockSpec((tm, d_model), lambda i: (i, 0)),
                   pl.BlockSpec((tm, d_model), lambda i: (i, 0))],
        out_shape=[jax.ShapeDtypeStruct((m, d_model), jnp.float32),
                   jax.ShapeDtypeStruct((m, d_model), jnp.bfloat16)],
        compiler_params=_compiler_params(("parallel",)),
        name="mix_ln1",
    )(x2d, o_dil2d, hcat, hcat, hcat, kv, w_sp, b_sp_t, sg_g, sg_b, w_out, g1, b1)


FFN_COL_SPLIT = 2


def _ffn_hidden_kernel(x_ref, wg_ref, wu_ref, side_ref, h_ref, side_out_ref, wgb_ref, wub_ref):
    @pl.when(pl.program_id(1) == 0)
    def _():
        wgb_ref[...] = wg_ref[...].astype(jnp.bfloat16)
        wub_ref[...] = wu_ref[...].astype(jnp.bfloat16)

    side_out_ref[...] = side_ref[...].astype(jnp.bfloat16)
    x = x_ref[...]
    bn = h_ref.shape[1]
    part = bn // FFN_COL_SPLIT
    for k in range(FFN_COL_SPLIT):
        cs = slice(k * part, (k + 1) * part)
        gate = jnp.dot(x, wgb_ref[:, cs], preferred_element_type=jnp.float32)
        up = jnp.dot(x, wub_ref[:, cs], preferred_element_type=jnp.float32)
        h_ref[:, cs] = (jax.nn.silu(gate) * up).astype(h_ref.dtype)


def _ffn_hidden(x1b, wg, wu, w_side, bm, bn):
    m, d_model = x1b.shape
    d_ff = wg.shape[1]
    grid = (d_ff // bn, m // bm)
    side_rows = w_side.shape[0] // (grid[0] * grid[1])
    side_spec = pl.BlockSpec((side_rows, w_side.shape[1]), lambda j, i: (j * grid[1] + i, 0))
    return pl.pallas_call(
        _ffn_hidden_kernel,
        grid=grid,
        in_specs=[pl.BlockSpec((bm, d_model), lambda j, i: (i, 0)),
                  pl.BlockSpec((d_model, bn), lambda j, i: (0, j)),
                  pl.BlockSpec((d_model, bn), lambda j, i: (0, j)),
                  side_spec],
        out_specs=[pl.BlockSpec((bm, bn), lambda j, i: (i, j)), side_spec],
        out_shape=[jax.ShapeDtypeStruct((m, d_ff), jnp.bfloat16),
                   jax.ShapeDtypeStruct(w_side.shape, jnp.bfloat16)],
        scratch_shapes=[pltpu.VMEM((d_model, bn), jnp.bfloat16),
                        pltpu.VMEM((d_model, bn), jnp.bfloat16)],
        compiler_params=_compiler_params(("parallel", "arbitrary")),
        name="ffn_hidden",
    )(x1b, wg, wu, w_side)


def _ffn_down_kernel(h_ref, x_ref, wd_ref, g2_ref, b2_ref, o_ref, f_ref):
    sub = h_ref.shape[0] // ROW_SPLIT
    row_parts = [slice(k * sub, (k + 1) * sub) for k in range(ROW_SPLIT)]
    for rows in row_parts:
        f_ref[rows, :] = jnp.dot(h_ref[rows, :], wd_ref[...], preferred_element_type=jnp.float32)
    for rows in row_parts:
        o_ref[rows, :] = _layer_norm(DEEPNORM_ALPHA * x_ref[rows, :] + f_ref[rows, :],
                                     g2_ref[...], b2_ref[...])


def _ffn_down(hid, x1, wd, g2, b2, tm):
    m, d_ff = hid.shape
    d_model = wd.shape[1]
    const = lambda i: (0, 0)
    return pl.pallas_call(
        _ffn_down_kernel,
        grid=(m // tm,),
        in_specs=[pl.BlockSpec((tm, d_ff), lambda i: (i, 0)),
                  pl.BlockSpec((tm, d_model), lambda i: (i, 0)),
                  _resident((d_ff, d_model), const),
                  _resident((1, d_model), const),
                  _resident((1, d_model), const)],
        out_specs=pl.BlockSpec((tm, d_model), lambda i: (i, 0)),
        out_shape=jax.ShapeDtypeStruct((m, d_model), jnp.float32),
        scratch_shapes=[pltpu.VMEM((tm, d_model), jnp.float32)],
        compiler_params=_compiler_params(("parallel",)),
        name="ffn_down_ln2",
    )(hid, x1, wd, g2, b2)


def kernel(x, mem, w_in, rel_bias, sg_ln_g, sg_ln_b, w_spatial, b_spatial, w_mem_kv, w_out,
           ln1_g, ln1_b, w_gate, w_up, w_down, ln2_g, ln2_b):
    batch, seq, d_model = x.shape
    n_mem = mem.shape[1]
    depth = w_in.shape[0]
    h2d = x.reshape(batch * seq, d_model)
    for l in range(depth):
        hcat = _proj(h2d, w_in[l].astype(jnp.bfloat16), bm=1024, bn=PROJ_BN)
        kv = _kv_proj(mem.reshape(batch * n_mem, d_model), w_mem_kv[l], bn=D_MEM_ATTN)
        o_dil, w_out_b = _dilated_attention(hcat, rel_bias, w_out[l], batch, seq)
        x1, x1b = _mix(h2d, o_dil.reshape(batch * seq, D_DIL), hcat,
                       kv.reshape(batch, n_mem, 2 * D_MEM_ATTN), w_spatial[l], b_spatial[l].T,
                       sg_ln_g[l][None], sg_ln_b[l][None], w_out_b,
                       ln1_g[l][None], ln1_b[l][None], seq, tm=512)
        hid, w_down_b = _ffn_hidden(x1b, w_gate[l], w_up[l], w_down[l], bm=2048, bn=512)
        h2d = _ffn_down(hid, x1, w_down_b, ln2_g[l][None], ln2_b[l][None], tm=512)
    return h2d.reshape(batch, seq, d_model)
```

```python
import math

import jax
import jax.numpy as jnp
import numpy as np
from jax import lax
from jax.experimental import pallas as pl
from jax.experimental.pallas import tpu as pltpu

D_MODEL = 2048
HEAD_DIM = 128
N_HEADS_DIL = 8
D_DIL = N_HEADS_DIL * HEAD_DIM
DILATIONS = (16, 4, 1)
N_STEPS = 128
BLOCK = 128
N_GROUPS_SG = 4
SG_CHUNK = 128
D_SG = N_GROUPS_SG * SG_CHUNK
N_HEADS_MEM = 4
D_MEM_ATTN = N_HEADS_MEM * HEAD_DIM
N_BUCKETS = 32
MAX_DISTANCE = 2048
DEEPNORM_ALPHA = 2.0 ** 0.25
LN_EPS = 1e-5
ATTN_SCALE = HEAD_DIM ** -0.5
LOG2_E = math.log2(math.e)
MASK_VALUE = -1e30
N_CLASSES = 16
GROUP = 4
SUBLANES = 8
ATTN_UNROLL = 16
COMBINE_ROWS = 64
PROJ_BN = 1536
ROW_SPLIT = 2

VMEM_LIMIT_BYTES = 58 * 1024 * 1024

_NT_DIMS = (((1,), (1,)), ((), ()))


def _layer_norm(y, g, b):
    mu = jnp.mean(y, axis=-1, keepdims=True)
    d = y - mu
    var = jnp.mean(d * d, axis=-1, keepdims=True)
    return d * lax.rsqrt(var + LN_EPS) * g + b


def _compiler_params(semantics):
    return pltpu.CompilerParams(dimension_semantics=semantics, vmem_limit_bytes=VMEM_LIMIT_BYTES)


def _resident(block_shape, index_map):
    return pl.BlockSpec(block_shape, index_map, pipeline_mode=pl.Buffered(1))


def _proj_kernel(x_ref, w_ref, o_ref, xb_ref):
    @pl.when(pl.program_id(1) == 0)
    def _():
        xb_ref[...] = x_ref[...].astype(jnp.bfloat16)

    bn = o_ref.shape[1]
    for jj in range(w_ref.shape[1] // bn):
        @pl.when(pl.program_id(1) == jj)
        def _(jj=jj):
            o_ref[...] = jnp.dot(xb_ref[...], w_ref[:, jj * bn:(jj + 1) * bn],
                                 preferred_element_type=jnp.float32)


def _proj(x2d, w_bf16, bm, bn):
    m, k = x2d.shape
    n = w_bf16.shape[1]
    return pl.pallas_call(
        _proj_kernel,
        grid=(m // bm, n // bn),
        in_specs=[pl.BlockSpec((bm, k), lambda i, j: (i, 0)),
                  _resident((k, n), lambda i, j: (0, 0))],
        out_specs=pl.BlockSpec((bm, bn), lambda i, j: (i, j)),
        out_shape=jax.ShapeDtypeStruct((m, n), jnp.float32),
        scratch_shapes=[pltpu.VMEM((bm, k), jnp.bfloat16)],
        compiler_params=_compiler_params(("parallel", "arbitrary")),
        name="proj",
    )(x2d, w_bf16)


def _kv_kernel(x_ref, w_ref, o_ref):
    o_ref[...] = jnp.dot(x_ref[...].astype(jnp.bfloat16), w_ref[...].astype(jnp.bfloat16),
                         preferred_element_type=jnp.float32).astype(o_ref.dtype)


def _kv_proj(mem2d, w, bn):
    m, k = mem2d.shape
    n = w.shape[1]
    return pl.pallas_call(
        _kv_kernel,
        grid=(n // bn,),
        in_specs=[pl.BlockSpec((m, k), lambda j: (0, 0)), pl.BlockSpec((k, bn), lambda j: (0, j))],
        out_specs=pl.BlockSpec((m, bn), lambda j: (0, j)),
        out_shape=jax.ShapeDtypeStruct((m, n), jnp.bfloat16),
        compiler_params=_compiler_params(("parallel",)),
        name="kv_proj",
    )(mem2d, w)


def _t5_bucket_starts():
    max_exact = N_BUCKETS // 2
    dist = np.arange(MAX_DISTANCE + 1)
    d = np.maximum(dist, 1).astype(np.float32)
    large = max_exact + (np.log(d / np.float32(max_exact)) / np.float32(math.log(MAX_DISTANCE / max_exact))
                         * np.float32(N_BUCKETS - max_exact)).astype(np.int32)
    bucket = np.where(dist < max_exact, dist, np.minimum(large, N_BUCKETS - 1))
    assert (np.diff(bucket) >= 0).all() and bucket[-1] == N_BUCKETS - 1
    return tuple(int(np.argmax(bucket >= b)) for b in range(N_BUCKETS))


_T5_BUCKET_STARTS = _t5_bucket_starts()


def _log2(n):
    assert n & (n - 1) == 0
    return n.bit_length() - 1


def _rows(start, rows, stride):
    if stride == 1:
        return pl.ds(start if isinstance(start, int) else pl.multiple_of(start, SUBLANES), rows)
    return pl.ds(start, rows, stride=stride)


def _block_plan(dil, blk, quarter):
    if dil == N_CLASSES:
        c, n = divmod(blk, 2)
        base = (c % GROUP) * quarter + c // GROUP
        q_runs = [(base + GROUP * BLOCK * n, BLOCK, GROUP)]
        k_runs = [(base, 2 * BLOCK, GROUP)]
        return q_runs, k_runs, int(n == 0)
    if dil == GROUP:
        blocks_per_sub = quarter // BLOCK
        g = lax.shift_right_logical(blk, _log2(blocks_per_sub))
        n = lax.bitwise_and(blk, blocks_per_sub - 1)
        first = jnp.where(n == 0, 1, 0)
        start = g * quarter + n * BLOCK
        return [(start, BLOCK, 1)], [(start - (BLOCK - first * BLOCK), 2 * BLOCK, 1)], first
    assert dil == 1
    q_rows = BLOCK // GROUP
    first = jnp.where(blk == 0, 1, 0)
    starts = [g * quarter + blk * q_rows for g in range(GROUP)]
    return ([(s, q_rows, 1) for s in starts],
            [(s - (q_rows - first * q_rows), 2 * q_rows, 1) for s in starts], first)


def _dil_attn_kernel(rb_ref, q_ref, k_ref, v_ref, side_ref, o_ref, side_out_ref,
                     qs_ref, ks_ref, vs_ref, acc_ref, m_ref, l_ref, bias_ref):
    seq = q_ref.shape[0]
    quarter = seq // GROUP
    h = pl.program_id(0)

    side_out_ref[...] = side_ref[...].astype(jnp.bfloat16)

    for g in range(GROUP):
        dst = slice(g * quarter, (g + 1) * quarter)
        src = pl.ds(g, quarter, stride=GROUP)
        qs_ref[dst, :] = q_ref[src, :] * (ATTN_SCALE * LOG2_E)
        ks_ref[dst, :] = k_ref[src, :]
        vs_ref[dst, :] = v_ref[src, :]

    @pl.when(pl.program_id(1) == 0)
    def _():
        row = lax.broadcasted_iota(jnp.int32, (BLOCK, 2 * BLOCK), 0)
        col = lax.broadcasted_iota(jnp.int32, (BLOCK, 2 * BLOCK), 1)
        for p, dil in enumerate(DILATIONS):
            n_cls = GROUP if dil == 1 else 1
            q_rows = BLOCK // n_cls
            k_rows = 2 * BLOCK // n_cls
            q_sub = (row % q_rows) * n_cls + row // q_rows
            k_sub = (col % k_rows) * n_cls + col // k_rows
            for variant, shift in enumerate((N_STEPS, 0)):
                steps = q_sub + shift - k_sub
                valid = (steps >= 0) & (steps <= N_STEPS)
                dist = jnp.maximum(steps, 0) * dil
                bias = jnp.full((BLOCK, 2 * BLOCK), rb_ref[0, h], jnp.float32)
                for b in range(1, N_BUCKETS):
                    if _T5_BUCKET_STARTS[b] <= N_STEPS * dil:
                        bias = jnp.where(dist >= _T5_BUCKET_STARTS[b], rb_ref[b, h], bias)
                bias_ref[2 * p + variant] = jnp.where(valid, bias * LOG2_E, MASK_VALUE)

    def attend(p, plans):
        staged = []
        for q_runs, k_runs, first in plans:
            qb = jnp.concatenate([qs_ref[_rows(*run), :] for run in q_runs], axis=0)
            kb = jnp.concatenate([ks_ref[_rows(*run), :] for run in k_runs], axis=0)
            s = lax.dot_general(qb.astype(jnp.bfloat16), kb.astype(jnp.bfloat16), _NT_DIMS,
                                preferred_element_type=jnp.float32)
            staged.append((q_runs, k_runs, s + bias_ref[2 * p + first]))
        softmaxed = []
        for q_runs, k_runs, s in staged:
            m_blk = jnp.max(s, axis=-1, keepdims=True)
            e = jnp.exp2(s - m_blk)
            l_blk = jnp.sum(e, axis=-1, keepdims=True)
            softmaxed.append((q_runs, k_runs, m_blk, l_blk, e.astype(jnp.bfloat16)))
        for q_runs, k_runs, m_blk, l_blk, e in softmaxed:
            vb = jnp.concatenate([vs_ref[_rows(*run), :] for run in k_runs], axis=0)
            pv = jnp.dot(e, vb.astype(jnp.bfloat16), preferred_element_type=jnp.float32)
            done = 0
            for run in q_runs:
                rows, sl = _rows(*run), slice(done, done + run[1])
                m_ref[p, rows, :] = jnp.broadcast_to(m_blk[sl], (run[1], HEAD_DIM))
                l_ref[p, rows, :] = jnp.broadcast_to(l_blk[sl], (run[1], HEAD_DIM))
                acc_ref[p, rows, :] = pv[sl]
                done += run[1]

    for p, dil in enumerate(DILATIONS):
        n_blocks = seq // BLOCK
        if dil == N_CLASSES:
            for n in range(2):
                attend(p, [_block_plan(dil, 2 * c + n, quarter) for c in range(N_CLASSES)])
        else:
            def blocks_body(it, carry, p=p, dil=dil):
                attend(p, [_block_plan(dil, it * ATTN_UNROLL + u, quarter) for u in range(ATTN_UNROLL)])
                return carry

            lax.fori_loop(0, n_blocks // ATTN_UNROLL, blocks_body, 0)

    for g in range(GROUP):
        for j0 in range(0, quarter, COMBINE_ROWS):
            rows = slice(g * quarter + j0, g * quarter + j0 + COMBINE_ROWS)
            maxes = [m_ref[p, rows, :] for p in range(len(DILATIONS))]
            m_all = jnp.maximum(jnp.maximum(maxes[0], maxes[1]), maxes[2])
            den = jnp.zeros((COMBINE_ROWS, HEAD_DIM), jnp.float32)
            num = jnp.zeros((COMBINE_ROWS, HEAD_DIM), jnp.float32)
            for p in range(len(DILATIONS)):
                w = jnp.exp2(maxes[p] - m_all)
                den = den + w * l_ref[p, rows, :]
                num = num + w * acc_ref[p, rows, :]
            o_ref[pl.ds(g + GROUP * j0, COMBINE_ROWS, stride=GROUP), :] = num / den


def _dilated_attention(hcat, rel_bias, w_side, batch, seq):
    hcat3 = hcat.reshape(batch, seq, hcat.shape[-1])
    blk = (None, seq, HEAD_DIM)
    rows = pltpu.VMEM((seq, HEAD_DIM), jnp.float32)
    per_pattern = pltpu.VMEM((len(DILATIONS), seq, HEAD_DIM), jnp.float32)
    side_rows = w_side.shape[0] // (N_HEADS_DIL * batch)
    side_spec = pl.BlockSpec((side_rows, w_side.shape[1]), lambda h, b: (h * batch + b, 0))
    return pl.pallas_call(
        _dil_attn_kernel,
        grid=(N_HEADS_DIL, batch),
        in_specs=[pl.BlockSpec(memory_space=pltpu.SMEM),
                  pl.BlockSpec(blk, lambda h, b: (b, 0, h)),
                  pl.BlockSpec(blk, lambda h, b: (b, 0, N_HEADS_DIL + h)),
                  pl.BlockSpec(blk, lambda h, b: (b, 0, 2 * N_HEADS_DIL + h)),
                  side_spec],
        out_specs=[pl.BlockSpec(blk, lambda h, b: (b, 0, h)), side_spec],
        out_shape=[jax.ShapeDtypeStruct((batch, seq, D_DIL), jnp.float32),
                   jax.ShapeDtypeStruct(w_side.shape, jnp.bfloat16)],
        scratch_shapes=[rows] * 3 + [per_pattern] * 3
        + [pltpu.VMEM((2 * len(DILATIONS), BLOCK, 2 * BLOCK), jnp.float32)],
        compiler_params=_compiler_params(("arbitrary", "arbitrary")),
        name="dilated_attn",
    )(rel_bias, hcat3, hcat3, hcat3, w_side)


def _mix_kernel(x_ref, odil_ref, u_ref, v_ref, qm_ref, kv_ref, wsp_ref, bsp_ref, sgg_ref, sgb_ref,
                wout_ref, g1_ref, b1_ref, o_ref, ob_ref):
    tm = x_ref.shape[0]

    r_i = lax.broadcasted_iota(jnp.int32, (SG_CHUNK, SG_CHUNK), 0)
    c_i = lax.broadcasted_iota(jnp.int32, (SG_CHUNK, SG_CHUNK), 1)
    causal = r_i >= c_i
    w_sp = [jnp.where(causal, wsp_ref[g], 0.0).astype(jnp.bfloat16) for g in range(N_GROUPS_SG)]
    bsp = bsp_ref[...]
    u = jax.nn.gelu(u_ref[...])
    v = _layer_norm(jax.nn.gelu(v_ref[...]), sgg_ref[...], sgb_ref[...]).astype(jnp.bfloat16)
    sg_rows = []
    for ci in range(tm // SG_CHUNK):
        rs = slice(ci * SG_CHUNK, (ci + 1) * SG_CHUNK)
        cols = []
        for g in range(N_GROUPS_SG):
            cs = slice(g * SG_CHUNK, (g + 1) * SG_CHUNK)
            mixed = jnp.dot(w_sp[g], v[rs, cs], preferred_element_type=jnp.float32)
            cols.append((u[rs, cs] * (mixed + bsp[:, g:g + 1])).astype(jnp.bfloat16))
        sg_rows.append(jnp.concatenate(cols, axis=1))
    o_sg = jnp.concatenate(sg_rows, axis=0)

    qm = qm_ref[...].astype(jnp.bfloat16)
    mem_cols = []
    for hh in range(N_HEADS_MEM):
        cs = slice(hh * HEAD_DIM, (hh + 1) * HEAD_DIM)
        kh = kv_ref[:, hh * HEAD_DIM:(hh + 1) * HEAD_DIM]
        vh = kv_ref[:, D_MEM_ATTN + hh * HEAD_DIM:D_MEM_ATTN + (hh + 1) * HEAD_DIM]
        s = lax.dot_general(qm[:, cs], kh, _NT_DIMS,
                            preferred_element_type=jnp.float32) * ATTN_SCALE
        s_max = jnp.max(s, axis=-1, keepdims=True)
        e = jnp.exp(s - s_max)
        den = jnp.sum(e, axis=-1, keepdims=True)
        o = jnp.dot(e.astype(jnp.bfloat16), vh, preferred_element_type=jnp.float32) / den
        mem_cols.append(o.astype(jnp.bfloat16))
    mix_in = jnp.concatenate([odil_ref[...].astype(jnp.bfloat16), o_sg] + mem_cols, axis=1)
    mix = jnp.dot(mix_in, wout_ref[...], preferred_element_type=jnp.float32)
    x1 = _layer_norm(DEEPNORM_ALPHA * x_ref[...] + mix, g1_ref[...], b1_ref[...])
    o_ref[...] = x1
    ob_ref[...] = x1.astype(jnp.bfloat16)


def _mix(x2d, o_dil2d, hcat, kv, w_sp, b_sp_t, sg_g, sg_b, w_out, g1, b1, seq, tm):
    m, d_model = x2d.shape
    tiles_per_batch = seq // tm
    n_mem = kv.shape[1]
    u_blk = 3 * D_DIL // D_SG
    const = lambda i: (0, 0)
    return pl.pallas_call(
        _mix_kernel,
        grid=(m // tm,),
        in_specs=[pl.BlockSpec((tm, d_model), lambda i: (i, 0)),
                  pl.BlockSpec((tm, D_DIL), lambda i: (i, 0)),
                  pl.BlockSpec((tm, D_SG), lambda i: (i, u_blk)),
                  pl.BlockSpec((tm, D_SG), lambda i: (i, u_blk + 1)),
                  pl.BlockSpec((tm, D_MEM_ATTN), lambda i: (i, u_blk + 2)),
                  pl.BlockSpec((None, n_mem, 2 * D_MEM_ATTN), lambda i: (i // tiles_per_batch, 0, 0)),
                  _resident((N_GROUPS_SG, SG_CHUNK, SG_CHUNK), lambda i: (0, 0, 0)),
                  _resident((SG_CHUNK, N_GROUPS_SG), const),
                  _resident((1, D_SG), const),
                  _resident((1, D_SG), const),
                  _resident((d_model, d_model), const),
                  _resident((1, d_model), const),
                  _resident((1, d_model), const)],
        out_specs=[pl.BlockSpec((tm, d_model), lambda i: (i, 0)),
                   pl.BlockSpec((tm, d_model), lambda i: (i, 0))],
        out_shape=[jax.ShapeDtypeStruct((m, d_model), jnp.float32),
                   jax.ShapeDtypeStruct((m, d_model), jnp.bfloat16)],
        compiler_params=_compiler_params(("parallel",)),
        name="mix_ln1",
    )(x2d, o_dil2d, hcat, hcat, hcat, kv, w_sp, b_sp_t, sg_g, sg_b, w_out, g1, b1)


FFN_COL_SPLIT = 2


def _ffn_hidden_kernel(x_ref, wg_ref, wu_ref, side_ref, h_ref, side_out_ref, wgb_ref, wub_ref):
    @pl.when(pl.program_id(1) == 0)
    def _():
        wgb_ref[...] = wg_ref[...].astype(jnp.bfloat16)
        wub_ref[...] = wu_ref[...].astype(jnp.bfloat16)

    side_out_ref[...] = side_ref[...].astype(jnp.bfloat16)
    x = x_ref[...]
    bn = h_ref.shape[1]
    part = bn // FFN_COL_SPLIT
    for k in range(FFN_COL_SPLIT):
        cs = slice(k * part, (k + 1) * part)
        gate = jnp.dot(x, wgb_ref[:, cs], preferred_element_type=jnp.float32)
        up = jnp.dot(x, wub_ref[:, cs], preferred_element_type=jnp.float32)
        h_ref[:, cs] = (jax.nn.silu(gate) * up).astype(h_ref.dtype)


def _ffn_hidden(x1b, wg, wu, w_side, bm, bn):
    m, d_model = x1b.shape
    d_ff = wg.shape[1]
    grid = (d_ff // bn, m // bm)
    side_rows = w_side.shape[0] // (grid[0] * grid[1])
    side_spec = pl.BlockSpec((side_rows, w_side.shape[1]), lambda j, i: (j * grid[1] + i, 0))
    return pl.pallas_call(
        _ffn_hidden_kernel,
        grid=grid,
        in_specs=[pl.BlockSpec((bm, d_model), lambda j, i: (i, 0)),
                  pl.BlockSpec((d_model, bn), lambda j, i: (0, j)),
                  pl.BlockSpec((d_model, bn), lambda j, i: (0, j)),
                  side_spec],
        out_specs=[pl.BlockSpec((bm, bn), lambda j, i: (i, j)), side_spec],
        out_shape=[jax.ShapeDtypeStruct((m, d_ff), jnp.bfloat16),
                   jax.ShapeDtypeStruct(w_side.shape, jnp.bfloat16)],
        scratch_shapes=[pltpu.VMEM((d_model, bn), jnp.bfloat16),
                        pltpu.VMEM((d_model, bn), jnp.bfloat16)],
        compiler_params=_compiler_params(("parallel", "arbitrary")),
        name="ffn_hidden",
    )(x1b, wg, wu, w_side)


def _ffn_down_kernel(h_ref, x_ref, wd_ref, g2_ref, b2_ref, o_ref, f_ref):
    sub = h_ref.shape[0] // ROW_SPLIT
    row_parts = [slice(k * sub, (k + 1) * sub) for k in range(ROW_SPLIT)]
    for rows in row_parts:
        f_ref[rows, :] = jnp.dot(h_ref[rows, :], wd_ref[...], preferred_element_type=jnp.float32)
        o_ref[rows, :] = _layer_norm(DEEPNORM_ALPHA * x_ref[rows, :] + f_ref[rows, :],
                                     g2_ref[...], b2_ref[...])


def _ffn_down(hid, x1, wd, g2, b2, tm):
    m, d_ff = hid.shape
    d_model = wd.shape[1]
    const = lambda i: (0, 0)
    return pl.pallas_call(
        _ffn_down_kernel,
        grid=(m // tm,),
        in_specs=[pl.BlockSpec((tm, d_ff), lambda i: (i, 0)),
                  pl.BlockSpec((tm, d_model), lambda i: (i, 0)),
                  _resident((d_ff, d_model), const),
                  _resident((1, d_model), const),
                  _resident((1, d_model), const)],
        out_specs=pl.BlockSpec((tm, d_model), lambda i: (i, 0)),
        out_shape=jax.ShapeDtypeStruct((m, d_model), jnp.float32),
        scratch_shapes=[pltpu.VMEM((tm, d_model), jnp.float32)],
        compiler_params=_compiler_params(("parallel",)),
        name="ffn_down_ln2",
    )(hid, x1, wd, g2, b2)


def kernel(x, mem, w_in, rel_bias, sg_ln_g, sg_ln_b, w_spatial, b_spatial, w_mem_kv, w_out,
           ln1_g, ln1_b, w_gate, w_up, w_down, ln2_g, ln2_b):
    batch, seq, d_model = x.shape
    n_mem = mem.shape[1]
    depth = w_in.shape[0]
    h2d = x.reshape(batch * seq, d_model)
    for l in range(depth):
        hcat = _proj(h2d, w_in[l].astype(jnp.bfloat16), bm=1024, bn=PROJ_BN)
        kv = _kv_proj(mem.reshape(batch * n_mem, d_model), w_mem_kv[l], bn=D_MEM_ATTN)
        o_dil, w_out_b = _dilated_attention(hcat, rel_bias, w_out[l], batch, seq)
        x1, x1b = _mix(h2d, o_dil.reshape(batch * seq, D_DIL), hcat,
                       kv.reshape(batch, n_mem, 2 * D_MEM_ATTN), w_spatial[l], b_spatial[l].T,
                       sg_ln_g[l][None], sg_ln_b[l][None], w_out_b,
                       ln1_g[l][None], ln1_b[l][None], seq, tm=512)
        hid, w_down_b = _ffn_hidden(x1b, w_gate[l], w_up[l], w_down[l], bm=2048, bn=512)
        h2d = _ffn_down(hid, x1, w_down_b, ln2_g[l][None], ln2_b[l][None], tm=512)
    return h2d.reshape(batch, seq, d_model)
```

```python
import math

import jax
import jax.numpy as jnp
import numpy as np
from jax import lax
from jax.experimental import pallas as pl
from jax.experimental.pallas import tpu as pltpu

D_MODEL = 2048
HEAD_DIM = 128
N_HEADS_DIL = 8
D_DIL = N_HEADS_DIL * HEAD_DIM
DILATIONS = (16, 4, 1)
N_STEPS = 128
BLOCK = 128
N_GROUPS_SG = 4
SG_CHUNK = 128
D_SG = N_GROUPS_SG * SG_CHUNK
N_HEADS_MEM = 4
D_MEM_ATTN = N_HEADS_MEM * HEAD_DIM
N_BUCKETS = 32
MAX_DISTANCE = 2048
DEEPNORM_ALPHA = 2.0 ** 0.25
LN_EPS = 1e-5
ATTN_SCALE = HEAD_DIM ** -0.5
LOG2_E = math.log2(math.e)
MASK_VALUE = -1e30
N_CLASSES = 16
GROUP = 4
SUBLANES = 8
ATTN_UNROLL = 16
COMBINE_ROWS = 64
PROJ_BN = 1536
ROW_SPLIT = 2

VMEM_LIMIT_BYTES = 58 * 1024 * 1024

_NT_DIMS = (((1,), (1,)), ((), ()))


def _layer_norm(y, g, b):
    mu = jnp.mean(y, axis=-1, keepdims=True)
    d = y - mu
    var = jnp.mean(d * d, axis=-1, keepdims=True)
    return d * lax.rsqrt(var + LN_EPS) * g + b


def _compiler_params(semantics):
    return pltpu.CompilerParams(dimension_semantics=semantics, vmem_limit_bytes=VMEM_LIMIT_BYTES)


def _resident(block_shape, index_map):
    return pl.BlockSpec(block_shape, index_map, pipeline_mode=pl.Buffered(1))


def _proj_kernel(x_ref, w_ref, o_ref, xb_ref):
    bn = o_ref.shape[1]
    for jj in range(w_ref.shape[1] // bn):
        @pl.when(pl.program_id(1) == jj)
        def _(jj=jj):
            if jj == 0:
                xb_ref[...] = x_ref[...].astype(jnp.bfloat16)
            o_ref[...] = jnp.dot(xb_ref[...], w_ref[:, jj * bn:(jj + 1) * bn],
                                 preferred_element_type=jnp.float32)


def _proj(x2d, w_bf16, bm, bn):
    m, k = x2d.shape
    n = w_bf16.shape[1]
    return pl.pallas_call(
        _proj_kernel,
        grid=(m // bm, n // bn),
        in_specs=[pl.BlockSpec((bm, k), lambda i, j: (i, 0)),
                  _resident((k, n), lambda i, j: (0, 0))],
        out_specs=pl.BlockSpec((bm, bn), lambda i, j: (i, j)),
        out_shape=jax.ShapeDtypeStruct((m, n), jnp.float32),
        scratch_shapes=[pltpu.VMEM((bm, k), jnp.bfloat16)],
        compiler_params=_compiler_params(("parallel", "arbitrary")),
        name="proj",
    )(x2d, w_bf16)


def _kv_kernel(x_ref, w_ref, o_ref):
    o_ref[...] = jnp.dot(x_ref[...].astype(jnp.bfloat16), w_ref[...].astype(jnp.bfloat16),
                         preferred_element_type=jnp.float32).astype(o_ref.dtype)


def _kv_proj(mem2d, w, bn):
    m, k = mem2d.shape
    n = w.shape[1]
    return pl.pallas_call(
        _kv_kernel,
        grid=(n // bn,),
        in_specs=[pl.BlockSpec((m, k), lambda j: (0, 0)), pl.BlockSpec((k, bn), lambda j: (0, j))],
        out_specs=pl.BlockSpec((m, bn), lambda j: (0, j)),
        out_shape=jax.ShapeDtypeStruct((m, n), jnp.bfloat16),
        compiler_params=_compiler_params(("parallel",)),
        name="kv_proj",
    )(mem2d, w)


def _t5_bucket_starts():
    max_exact = N_BUCKETS // 2
    dist = np.arange(MAX_DISTANCE + 1)
    d = np.maximum(dist, 1).astype(np.float32)
    large = max_exact + (np.log(d / np.float32(max_exact)) / np.float32(math.log(MAX_DISTANCE / max_exact))
                         * np.float32(N_BUCKETS - max_exact)).astype(np.int32)
    bucket = np.where(dist < max_exact, dist, np.minimum(large, N_BUCKETS - 1))
    assert (np.diff(bucket) >= 0).all() and bucket[-1] == N_BUCKETS - 1
    return tuple(int(np.argmax(bucket >= b)) for b in range(N_BUCKETS))


_T5_BUCKET_STARTS = _t5_bucket_starts()


def _log2(n):
    assert n & (n - 1) == 0
    return n.bit_length() - 1


def _rows(start, rows, stride):
    if stride == 1:
        return pl.ds(start if isinstance(start, int) else pl.multiple_of(start, SUBLANES), rows)
    return pl.ds(start, rows, stride=stride)


def _block_plan(dil, blk, quarter):
    if dil == N_CLASSES:
        c, n = divmod(blk, 2)
        base = (c % GROUP) * quarter + c // GROUP
        q_runs = [(base + GROUP * BLOCK * n, BLOCK, GROUP)]
        k_runs = [(base, 2 * BLOCK, GROUP)]
        return q_runs, k_runs, int(n == 0)
    if dil == GROUP:
        blocks_per_sub = quarter // BLOCK
        g = lax.shift_right_logical(blk, _log2(blocks_per_sub))
        n = lax.bitwise_and(blk, blocks_per_sub - 1)
        first = jnp.where(n == 0, 1, 0)
        start = g * quarter + n * BLOCK
        return [(start, BLOCK, 1)], [(start - (BLOCK - first * BLOCK), 2 * BLOCK, 1)], first
    assert dil == 1
    q_rows = BLOCK // GROUP
    first = jnp.where(blk == 0, 1, 0)
    starts = [g * quarter + blk * q_rows for g in range(GROUP)]
    return ([(s, q_rows, 1) for s in starts],
            [(s - (q_rows - first * q_rows), 2 * q_rows, 1) for s in starts], first)


def _dil_attn_kernel(rb_ref, q_ref, k_ref, v_ref, side_ref, o_ref, side_out_ref,
                     qs_ref, ks_ref, vs_ref, acc_ref, m_ref, l_ref, bias_ref):
    seq = q_ref.shape[0]
    quarter = seq // GROUP
    h = pl.program_id(0)

    side_out_ref[...] = side_ref[...].astype(jnp.bfloat16)

    for g in range(GROUP):
        dst = slice(g * quarter, (g + 1) * quarter)
        src = pl.ds(g, quarter, stride=GROUP)
        qs_ref[dst, :] = q_ref[src, :] * (ATTN_SCALE * LOG2_E)
        ks_ref[dst, :] = k_ref[src, :]
        vs_ref[dst, :] = v_ref[src, :]

    @pl.when(pl.program_id(1) == 0)
    def _():
        row = lax.broadcasted_iota(jnp.int32, (BLOCK, 2 * BLOCK), 0)
        col = lax.broadcasted_iota(jnp.int32, (BLOCK, 2 * BLOCK), 1)
        for p, dil in enumerate(DILATIONS):
            n_cls = GROUP if dil == 1 else 1
            q_rows = BLOCK // n_cls
            k_rows = 2 * BLOCK // n_cls
            q_sub = (row % q_rows) * n_cls + row // q_rows
            k_sub = (col % k_rows) * n_cls + col // k_rows
            for variant, shift in enumerate((N_STEPS, 0)):
                steps = q_sub + shift - k_sub
                valid = (steps >= 0) & (steps <= N_STEPS)
                dist = jnp.maximum(steps, 0) * dil
                bias = jnp.full((BLOCK, 2 * BLOCK), rb_ref[0, h], jnp.float32)
                for b in range(1, N_BUCKETS):
                    if _T5_BUCKET_STARTS[b] <= N_STEPS * dil:
                        bias = jnp.where(dist >= _T5_BUCKET_STARTS[b], rb_ref[b, h], bias)
                bias_ref[2 * p + variant] = jnp.where(valid, bias * LOG2_E, MASK_VALUE)

    def attend(p, plans):
        staged = []
        for q_runs, k_runs, first in plans:
            qb = jnp.concatenate([qs_ref[_rows(*run), :] for run in q_runs], axis=0)
            kb = jnp.concatenate([ks_ref[_rows(*run), :] for run in k_runs], axis=0)
            s = lax.dot_general(qb.astype(jnp.bfloat16), kb.astype(jnp.bfloat16), _NT_DIMS,
                                preferred_element_type=jnp.float32)
            staged.append((q_runs, k_runs, s + bias_ref[2 * p + first]))
        softmaxed = []
        for q_runs, k_runs, s in staged:
            m_blk = jnp.max(s, axis=-1, keepdims=True)
            e = jnp.exp2(s - m_blk)
            l_blk = jnp.sum(e, axis=-1, keepdims=True)
            softmaxed.append((q_runs, k_runs, m_blk, l_blk, e.astype(jnp.bfloat16)))
        for q_runs, k_runs, m_blk, l_blk, e in softmaxed:
            vb = jnp.concatenate([vs_ref[_rows(*run), :] for run in k_runs], axis=0)
            pv = jnp.dot(e, vb.astype(jnp.bfloat16), preferred_element_type=jnp.float32)
            done = 0
            for run in q_runs:
                rows, sl = _rows(*run), slice(done, done + run[1])
                m_ref[p, rows, :] = jnp.broadcast_to(m_blk[sl], (run[1], HEAD_DIM))
                l_ref[p, rows, :] = jnp.broadcast_to(l_blk[sl], (run[1], HEAD_DIM))
                acc_ref[p, rows, :] = pv[sl]
                done += run[1]

    for p, dil in enumerate(DILATIONS):
        n_blocks = seq // BLOCK
        if dil == N_CLASSES:
            for n in range(2):
                attend(p, [_block_plan(dil, 2 * c + n, quarter) for c in range(N_CLASSES)])
        else:
            def blocks_body(it, carry, p=p, dil=dil):
                attend(p, [_block_plan(dil, it * ATTN_UNROLL + u, quarter) for u in range(ATTN_UNROLL)])
                return carry

            lax.fori_loop(0, n_blocks // ATTN_UNROLL, blocks_body, 0)

    for g in range(GROUP):
        for j0 in range(0, quarter, COMBINE_ROWS):
            rows = slice(g * quarter + j0, g * quarter + j0 + COMBINE_ROWS)
            maxes = [m_ref[p, rows, :] for p in range(len(DILATIONS))]
            m_all = jnp.maximum(jnp.maximum(maxes[0], maxes[1]), maxes[2])
            den = jnp.zeros((COMBINE_ROWS, HEAD_DIM), jnp.float32)
            num = jnp.zeros((COMBINE_ROWS, HEAD_DIM), jnp.float32)
            for p in range(len(DILATIONS)):
                w = jnp.exp2(maxes[p] - m_all)
                den = den + w * l_ref[p, rows, :]
                num = num + w * acc_ref[p, rows, :]
            o_ref[pl.ds(g + GROUP * j0, COMBINE_ROWS, stride=GROUP), :] = num / den


def _dilated_attention(hcat, rel_bias, w_side, batch, seq):
    hcat3 = hcat.reshape(batch, seq, hcat.shape[-1])
    blk = (None, seq, HEAD_DIM)
    rows = pltpu.VMEM((seq, HEAD_DIM), jnp.float32)
    per_pattern = pltpu.VMEM((len(DILATIONS), seq, HEAD_DIM), jnp.float32)
    side_rows = w_side.shape[0] // (N_HEADS_DIL * batch)
    side_spec = pl.BlockSpec((side_rows, w_side.shape[1]), lambda h, b: (h * batch + b, 0))
    return pl.pallas_call(
        _dil_attn_kernel,
        grid=(N_HEADS_DIL, batch),
        in_specs=[pl.BlockSpec(memory_space=pltpu.SMEM),
                  pl.BlockSpec(blk, lambda h, b: (b, 0, h)),
                  pl.BlockSpec(blk, lambda h, b: (b, 0, N_HEADS_DIL + h)),
                  pl.BlockSpec(blk, lambda h, b: (b, 0, 2 * N_HEADS_DIL + h)),
                  side_spec],
        out_specs=[pl.BlockSpec(blk, lambda h, b: (b, 0, h)), side_spec],
        out_shape=[jax.ShapeDtypeStruct((batch, seq, D_DIL), jnp.float32),
                   jax.ShapeDtypeStruct(w_side.shape, jnp.bfloat16)],
        scratch_shapes=[rows] * 3 + [per_pattern] * 3
        + [pltpu.VMEM((2 * len(DILATIONS), BLOCK, 2 * BLOCK), jnp.float32)],
        compiler_params=_compiler_params(("arbitrary", "arbitrary")),
        name="dilated_attn",
    )(rel_bias, hcat3, hcat3, hcat3, w_side)


def _mix_kernel(x_ref, odil_ref, u_ref, v_ref, qm_ref, kv_ref, wsp_ref, bsp_ref, sgg_ref, sgb_ref,
                wout_ref, g1_ref, b1_ref, o_ref, ob_ref):
    tm = x_ref.shape[0]

    r_i = lax.broadcasted_iota(jnp.int32, (SG_CHUNK, SG_CHUNK), 0)
    c_i = lax.broadcasted_iota(jnp.int32, (SG_CHUNK, SG_CHUNK), 1)
    causal = r_i >= c_i
    w_sp = [jnp.where(causal, wsp_ref[g], 0.0).astype(jnp.bfloat16) for g in range(N_GROUPS_SG)]
    bsp = bsp_ref[...]
    u = jax.nn.gelu(u_ref[...])
    v = _layer_norm(jax.nn.gelu(v_ref[...]), sgg_ref[...], sgb_ref[...]).astype(jnp.bfloat16)
    sg_rows = []
    for ci in range(tm // SG_CHUNK):
        rs = slice(ci * SG_CHUNK, (ci + 1) * SG_CHUNK)
        cols = []
        for g in range(N_GROUPS_SG):
            cs = slice(g * SG_CHUNK, (g + 1) * SG_CHUNK)
            mixed = jnp.dot(w_sp[g], v[rs, cs], preferred_element_type=jnp.float32)
            cols.append((u[rs, cs] * (mixed + bsp[:, g:g + 1])).astype(jnp.bfloat16))
        sg_rows.append(jnp.concatenate(cols, axis=1))
    o_sg = jnp.concatenate(sg_rows, axis=0)

    qm = qm_ref[...].astype(jnp.bfloat16)
    mem_cols = []
    for hh in range(N_HEADS_MEM):
        cs = slice(hh * HEAD_DIM, (hh + 1) * HEAD_DIM)
        kh = kv_ref[:, hh * HEAD_DIM:(hh + 1) * HEAD_DIM]
        vh = kv_ref[:, D_MEM_ATTN + hh * HEAD_DIM:D_MEM_ATTN + (hh + 1) * HEAD_DIM]
        s = lax.dot_general(qm[:, cs], kh, _NT_DIMS,
                            preferred_element_type=jnp.float32) * ATTN_SCALE
        s_max = jnp.max(s, axis=-1, keepdims=True)
        e = jnp.exp(s - s_max)
        den = jnp.sum(e, axis=-1, keepdims=True)
        o = jnp.dot(e.astype(jnp.bfloat16), vh, preferred_element_type=jnp.float32) / den
        mem_cols.append(o.astype(jnp.bfloat16))
    mix_in = jnp.concatenate([odil_ref[...].astype(jnp.bfloat16), o_sg] + mem_cols, axis=1)
    mix = jnp.dot(mix_in, wout_ref[...], preferred_element_type=jnp.float32)
    x1 = _layer_norm(DEEPNORM_ALPHA * x_ref[...] + mix, g1_ref[...], b1_ref[...])
    o_ref[...] = x1
    ob_ref[...] = x1.astype(jnp.bfloat16)


def _mix(x2d, o_dil2d, hcat, kv, w_sp, b_sp_t, sg_g, sg_b, w_out, g1, b1, seq, tm):
    m, d_model = x2d.shape
    tiles_per_batch = seq // tm
    n_mem = kv.shape[1]
    u_blk = 3 * D_DIL // D_SG
    const = lambda i: (0, 0)
    return pl.pallas_call(
        _mix_kernel,
        grid=(m // tm,),
        in_specs=[pl.BlockSpec((tm, d_model), lambda i: (i, 0)),
                  pl.BlockSpec((tm, D_DIL), lambda i: (i, 0)),
                  pl.BlockSpec((tm, D_SG), lambda i: (i, u_blk)),
                  pl.BlockSpec((tm, D_SG), lambda i: (i, u_blk + 1)),
                  pl.BlockSpec((tm, D_MEM_ATTN), lambda i: (i, u_blk + 2)),
                  pl.BlockSpec((None, n_mem, 2 * D_MEM_ATTN), lambda i: (i // tiles_per_batch, 0, 0)),
                  _resident((N_GROUPS_SG, SG_CHUNK, SG_CHUNK), lambda i: (0, 0, 0)),
                  _resident((SG_CHUNK, N_GROUPS_SG), const),
                  _resident((1, D_SG), const),
                  _resident((1, D_SG), const),
                  _resident((d_model, d_model), const),
                  _resident((1, d_model), const),
                  _resident((1, d_model), const)],
        out_specs=[pl.BlockSpec((tm, d_model), lambda i: (i, 0)),
                   pl.BlockSpec((tm, d_model), lambda i: (i, 0))],
        out_shape=[jax.ShapeDtypeStruct((m, d_model), jnp.float32),
                   jax.ShapeDtypeStruct((m, d_model), jnp.bfloat16)],
        compiler_params=_compiler_params(("parallel",)),
        name="mix_ln1",
    )(x2d, o_dil2d, hcat, hcat, hcat, kv, w_sp, b_sp_t, sg_g, sg_b, w_out, g1, b1)


FFN_COL_SPLIT = 2


def _ffn_hidden_kernel(x_ref, wg_ref, wu_ref, side_ref, h_ref, side_out_ref, wgb_ref, wub_ref):
    side_out_ref[...] = side_ref[...].astype(jnp.bfloat16)
    bn = h_ref.shape[1]
    part = bn // FFN_COL_SPLIT

    def hidden(refresh_weights):
        x = x_ref[...]
        for k in range(FFN_COL_SPLIT):
            cs = slice(k * part, (k + 1) * part)
            if refresh_weights:
                wgb_ref[:, cs] = wg_ref[:, cs].astype(jnp.bfloat16)
                wub_ref[:, cs] = wu_ref[:, cs].astype(jnp.bfloat16)
            gate = jnp.dot(x, wgb_ref[:, cs], preferred_element_type=jnp.float32)
            up = jnp.dot(x, wub_ref[:, cs], preferred_element_type=jnp.float32)
            h_ref[:, cs] = (jax.nn.silu(gate) * up).astype(h_ref.dtype)

    @pl.when(pl.program_id(1) == 0)
    def _():
        hidden(refresh_weights=True)

    @pl.when(pl.program_id(1) != 0)
    def _():
        hidden(refresh_weights=False)


def _ffn_hidden(x1b, wg, wu, w_side, bm, bn):
    m, d_model = x1b.shape
    d_ff = wg.shape[1]
    grid = (d_ff // bn, m // bm)
    side_rows = w_side.shape[0] // (grid[0] * grid[1])
    side_spec = pl.BlockSpec((side_rows, w_side.shape[1]), lambda j, i: (j * grid[1] + i, 0))
    return pl.pallas_call(
        _ffn_hidden_kernel,
        grid=grid,
        in_specs=[pl.BlockSpec((bm, d_model), lambda j, i: (i, 0)),
                  pl.BlockSpec((d_model, bn), lambda j, i: (0, j)),
                  pl.BlockSpec((d_model, bn), lambda j, i: (0, j)),
                  side_spec],
        out_specs=[pl.BlockSpec((bm, bn), lambda j, i: (i, j)), side_spec],
        out_shape=[jax.ShapeDtypeStruct((m, d_ff), jnp.bfloat16),
                   jax.ShapeDtypeStruct(w_side.shape, jnp.bfloat16)],
        scratch_shapes=[pltpu.VMEM((d_model, bn), jnp.bfloat16),
                        pltpu.VMEM((d_model, bn), jnp.bfloat16)],
        compiler_params=_compiler_params(("parallel", "arbitrary")),
        name="ffn_hidden",
    )(x1b, wg, wu, w_side)


def _ffn_down_kernel(h_ref, x_ref, wd_ref, g2_ref, b2_ref, o_ref, f_ref):
    sub = h_ref.shape[0] // ROW_SPLIT
    row_parts = [slice(k * sub, (k + 1) * sub) for k in range(ROW_SPLIT)]
    for rows in row_parts:
        f_ref[rows, :] = jnp.dot(h_ref[rows, :], wd_ref[...], preferred_element_type=jnp.float32)
        o_ref[rows, :] = _layer_norm(DEEPNORM_ALPHA * x_ref[rows, :] + f_ref[rows, :],
                                     g2_ref[...], b2_ref[...])


def _ffn_down(hid, x1, wd, g2, b2, tm):
    m, d_ff = hid.shape
    d_model = wd.shape[1]
    const = lambda i: (0, 0)
    return pl.pallas_call(
        _ffn_down_kernel,
        grid=(m // tm,),
        in_specs=[pl.BlockSpec((tm, d_ff), lambda i: (i, 0)),
                  pl.BlockSpec((tm, d_model), lambda i: (i, 0)),
                  _resident((d_ff, d_model), const),
                  _resident((1, d_model), const),
                  _resident((1, d_model), const)],
        out_specs=pl.BlockSpec((tm, d_model), lambda i: (i, 0)),
        out_shape=jax.ShapeDtypeStruct((m, d_model), jnp.float32),
        scratch_shapes=[pltpu.VMEM((tm, d_model), jnp.float32)],
        compiler_params=_compiler_params(("parallel",)),
        name="ffn_down_ln2",
    )(hid, x1, wd, g2, b2)


def kernel(x, mem, w_in, rel_bias, sg_ln_g, sg_ln_b, w_spatial, b_spatial, w_mem_kv, w_out,
           ln1_g, ln1_b, w_gate, w_up, w_down, ln2_g, ln2_b):
    batch, seq, d_model = x.shape
    n_mem = mem.shape[1]
    depth = w_in.shape[0]
    h2d = x.reshape(batch * seq, d_model)
    for l in range(depth):
        hcat = _proj(h2d, w_in[l].astype(jnp.bfloat16), bm=1024, bn=PROJ_BN)
        kv = _kv_proj(mem.reshape(batch * n_mem, d_model), w_mem_kv[l], bn=D_MEM_ATTN)
        o_dil, w_out_b = _dilated_attention(hcat, rel_bias, w_out[l], batch, seq)
        x1, x1b = _mix(h2d, o_dil.reshape(batch * seq, D_DIL), hcat,
                       kv.reshape(batch, n_mem, 2 * D_MEM_ATTN), w_spatial[l], b_spatial[l].T,
                       sg_ln_g[l][None], sg_ln_b[l][None], w_out_b,
                       ln1_g[l][None], ln1_b[l][None], seq, tm=512)
        hid, w_down_b = _ffn_hidden(x1b, w_gate[l], w_up[l], w_down[l], bm=2048, bn=512)
        h2d = _ffn_down(hid, x1, w_down_b, ln2_g[l][None], ln2_b[l][None], tm=512)
    return h2d.reshape(batch, seq, d_model)
```

```python
import math

import jax
import jax.numpy as jnp
import numpy as np
from jax import lax
from jax.experimental import pallas as pl
from jax.experimental.pallas import tpu as pltpu

D_MODEL = 2048
HEAD_DIM = 128
N_HEADS_DIL = 8
D_DIL = N_HEADS_DIL * HEAD_DIM
DILATIONS = (16, 4, 1)
N_STEPS = 128
BLOCK = 128
N_GROUPS_SG = 4
SG_CHUNK = 128
D_SG = N_GROUPS_SG * SG_CHUNK
N_HEADS_MEM = 4
D_MEM_ATTN = N_HEADS_MEM * HEAD_DIM
N_BUCKETS = 32
MAX_DISTANCE = 2048
DEEPNORM_ALPHA = 2.0 ** 0.25
LN_EPS = 1e-5
ATTN_SCALE = HEAD_DIM ** -0.5
LOG2_E = math.log2(math.e)
MASK_VALUE = float("-inf")
N_CLASSES = 16
GROUP = 4
ATTN_UNROLL = 16
COMBINE_ROWS = 64
PROJ_BN = 1536
ROW_SPLIT = 2

VMEM_LIMIT_BYTES = 58 * 1024 * 1024

_NT_DIMS = (((1,), (1,)), ((), ()))


def _layer_norm(y, g, b):
    mu = jnp.mean(y, axis=-1, keepdims=True)
    d = y - mu
    var = jnp.mean(d * d, axis=-1, keepdims=True)
    return d * lax.rsqrt(var + LN_EPS) * g + b


def _compiler_params(semantics):
    return pltpu.CompilerParams(dimension_semantics=semantics, vmem_limit_bytes=VMEM_LIMIT_BYTES)


def _resident(block_shape, index_map):
    return pl.BlockSpec(block_shape, index_map, pipeline_mode=pl.Buffered(1))


def _proj_kernel(x_ref, w_ref, o_ref, xb_ref):
    bn = o_ref.shape[1]
    for jj in range(w_ref.shape[1] // bn):
        @pl.when(pl.program_id(1) == jj)
        def _(jj=jj):
            if jj == 0:
                xb_ref[...] = x_ref[...].astype(jnp.bfloat16)
            o_ref[...] = jnp.dot(xb_ref[...], w_ref[:, jj * bn:(jj + 1) * bn],
                                 preferred_element_type=jnp.float32)


def _proj(x2d, w_bf16, bm, bn):
    m, k = x2d.shape
    n = w_bf16.shape[1]
    return pl.pallas_call(
        _proj_kernel,
        grid=(m // bm, n // bn),
        in_specs=[pl.BlockSpec((bm, k), lambda i, j: (i, 0)),
                  _resident((k, n), lambda i, j: (0, 0))],
        out_specs=pl.BlockSpec((bm, bn), lambda i, j: (i, j)),
        out_shape=jax.ShapeDtypeStruct((m, n), jnp.float32),
        scratch_shapes=[pltpu.VMEM((bm, k), jnp.bfloat16)],
        compiler_params=_compiler_params(("parallel", "arbitrary")),
        name="proj",
    )(x2d, w_bf16)


def _kv_kernel(x_ref, w_ref, o_ref):
    o_ref[...] = jnp.dot(x_ref[...].astype(jnp.bfloat16), w_ref[...].astype(jnp.bfloat16),
                         preferred_element_type=jnp.float32).astype(o_ref.dtype)


def _kv_proj(mem2d, w, bn):
    m, k = mem2d.shape
    n = w.shape[1]
    return pl.pallas_call(
        _kv_kernel,
        grid=(n // bn,),
        in_specs=[pl.BlockSpec((m, k), lambda j: (0, 0)), pl.BlockSpec((k, bn), lambda j: (0, j))],
        out_specs=pl.BlockSpec((m, bn), lambda j: (0, j)),
        out_shape=jax.ShapeDtypeStruct((m, n), jnp.bfloat16),
        compiler_params=_compiler_params(("parallel",)),
        name="kv_proj",
    )(mem2d, w)


def _t5_bucket_starts():
    max_exact = N_BUCKETS // 2
    dist = np.arange(MAX_DISTANCE + 1)
    d = np.maximum(dist, 1).astype(np.float32)
    large = max_exact + (np.log(d / np.float32(max_exact)) / np.float32(math.log(MAX_DISTANCE / max_exact))
                         * np.float32(N_BUCKETS - max_exact)).astype(np.int32)
    bucket = np.where(dist < max_exact, dist, np.minimum(large, N_BUCKETS - 1))
    assert (np.diff(bucket) >= 0).all() and bucket[-1] == N_BUCKETS - 1
    return tuple(int(np.argmax(bucket >= b)) for b in range(N_BUCKETS))


_T5_BUCKET_STARTS = _t5_bucket_starts()


def _rows(start, rows, stride):
    return pl.ds(start, rows) if stride == 1 else pl.ds(start, rows, stride=stride)


def _block_plan(dil, blk, quarter):
    if dil == N_CLASSES:
        c, n = divmod(blk, quarter * GROUP // N_CLASSES // BLOCK)
        base, stride = (c % GROUP) * quarter + c // GROUP, GROUP
    elif dil == GROUP:
        g, n = divmod(blk, quarter // BLOCK)
        base, stride = g * quarter, 1
    else:
        assert dil == 1
        q_rows, first = BLOCK // GROUP, int(blk == 0)
        starts = [g * quarter + blk * q_rows for g in range(GROUP)]
        return ([(s, q_rows, 1) for s in starts],
                [(s - (1 - first) * q_rows, 2 * q_rows, 1) for s in starts], first)
    first = int(n == 0)
    q_runs = [(base + stride * BLOCK * n, BLOCK, stride)]
    k_runs = [(base, BLOCK, stride)] if first else [(base + stride * BLOCK * (n - 1), 2 * BLOCK, stride)]
    return q_runs, k_runs, first


def _dil_attn_kernel(rb_ref, q_ref, k_ref, v_ref, side_ref, o_ref, side_out_ref,
                     qs_ref, ks_ref, vs_ref, acc_ref, m_ref, l_ref, bias_ref):
    seq = q_ref.shape[0]
    quarter = seq // GROUP
    h = pl.program_id(0)

    side_out_ref[...] = side_ref[...].astype(jnp.bfloat16)

    def regroup():
        for g in range(GROUP):
            dst = slice(g * quarter, (g + 1) * quarter)
            src = pl.ds(g, quarter, stride=GROUP)
            qs_ref[dst, :] = q_ref[src, :] * (ATTN_SCALE * LOG2_E)
            ks_ref[dst, :] = k_ref[src, :]
            vs_ref[dst, :] = v_ref[src, :]

    def build_bias():
        row = lax.broadcasted_iota(jnp.int32, (BLOCK, 2 * BLOCK), 0)
        col = lax.broadcasted_iota(jnp.int32, (BLOCK, 2 * BLOCK), 1)
        for p, dil in enumerate(DILATIONS):
            n_cls = GROUP if dil == 1 else 1
            q_rows = BLOCK // n_cls
            k_rows = 2 * BLOCK // n_cls
            q_sub = (row % q_rows) * n_cls + row // q_rows
            k_sub = (col % k_rows) * n_cls + col // k_rows
            for variant, shift in enumerate((N_STEPS, 0)):
                steps = q_sub + shift - k_sub
                valid = (steps >= 0) & (steps <= N_STEPS)
                dist = jnp.maximum(steps, 0) * dil
                bias = jnp.full((BLOCK, 2 * BLOCK), rb_ref[0, h], jnp.float32)
                for b in range(1, N_BUCKETS):
                    if _T5_BUCKET_STARTS[b] <= N_STEPS * dil:
                        bias = jnp.where(dist >= _T5_BUCKET_STARTS[b], rb_ref[b, h], bias)
                bias_ref[2 * p + variant] = jnp.where(valid, bias * LOG2_E, MASK_VALUE)

    def attend(p, plans):
        staged = []
        for q_runs, k_runs, first in plans:
            qb = jnp.concatenate([qs_ref[_rows(*run), :] for run in q_runs], axis=0)
            kb = jnp.concatenate([ks_ref[_rows(*run), :] for run in k_runs], axis=0)
            s = lax.dot_general(qb.astype(jnp.bfloat16), kb.astype(jnp.bfloat16), _NT_DIMS,
                                preferred_element_type=jnp.float32)
            n_keys = sum(run[1] for run in k_runs)
            staged.append((q_runs, k_runs, s + bias_ref[2 * p + first, :, :n_keys]))
        softmaxed = []
        for q_runs, k_runs, s in staged:
            m_blk = jnp.max(s, axis=-1, keepdims=True)
            e = jnp.exp2(s - m_blk)
            l_blk = jnp.sum(e, axis=-1, keepdims=True)
            softmaxed.append((q_runs, k_runs, m_blk, l_blk, e.astype(jnp.bfloat16)))
        for q_runs, k_runs, m_blk, l_blk, e in softmaxed:
            vb = jnp.concatenate([vs_ref[_rows(*run), :] for run in k_runs], axis=0)
            pv = jnp.dot(e, vb.astype(jnp.bfloat16), preferred_element_type=jnp.float32)
            done = 0
            for run in q_runs:
                rows, sl = _rows(*run), slice(done, done + run[1])
                m_ref[p, rows, :] = jnp.broadcast_to(m_blk[sl], (run[1], HEAD_DIM))
                l_ref[p, rows, :] = jnp.broadcast_to(l_blk[sl], (run[1], HEAD_DIM))
                acc_ref[p, rows, :] = pv[sl]
                done += run[1]

    def attend_all_and_combine():
        for p, dil in enumerate(DILATIONS):
            blocks = list(range(seq // BLOCK))
            if dil == N_CLASSES:
                blocks = blocks[0::2] + blocks[1::2]
            for i in range(0, len(blocks), ATTN_UNROLL):
                attend(p, [_block_plan(dil, blk, quarter) for blk in blocks[i:i + ATTN_UNROLL]])

        for g in range(GROUP):
            for j0 in range(0, quarter, COMBINE_ROWS):
                rows = slice(g * quarter + j0, g * quarter + j0 + COMBINE_ROWS)
                maxes = [m_ref[p, rows, :] for p in range(len(DILATIONS))]
                m_all = jnp.maximum(jnp.maximum(maxes[0], maxes[1]), maxes[2])
                den = jnp.zeros((COMBINE_ROWS, HEAD_DIM), jnp.float32)
                num = jnp.zeros((COMBINE_ROWS, HEAD_DIM), jnp.float32)
                for p in range(len(DILATIONS)):
                    w = jnp.exp2(maxes[p] - m_all)
                    den = den + w * l_ref[p, rows, :]
                    num = num + w * acc_ref[p, rows, :]
                o_ref[pl.ds(g + GROUP * j0, COMBINE_ROWS, stride=GROUP), :] = num / den

    @pl.when(pl.program_id(1) == 0)
    def _():
        regroup()
        build_bias()
        attend_all_and_combine()

    @pl.when(pl.program_id(1) != 0)
    def _():
        regroup()
        attend_all_and_combine()


def _dilated_attention(hcat, rel_bias, w_side, batch, seq):
    hcat3 = hcat.reshape(batch, seq, hcat.shape[-1])
    blk = (None, seq, HEAD_DIM)
    rows = pltpu.VMEM((seq, HEAD_DIM), jnp.float32)
    per_pattern = pltpu.VMEM((len(DILATIONS), seq, HEAD_DIM), jnp.float32)
    side_rows = w_side.shape[0] // (N_HEADS_DIL * batch)
    side_spec = pl.BlockSpec((side_rows, w_side.shape[1]), lambda h, b: (h * batch + b, 0))
    return pl.pallas_call(
        _dil_attn_kernel,
        grid=(N_HEADS_DIL, batch),
        in_specs=[pl.BlockSpec(memory_space=pltpu.SMEM),
                  pl.BlockSpec(blk, lambda h, b: (b, 0, h)),
                  pl.BlockSpec(blk, lambda h, b: (b, 0, N_HEADS_DIL + h)),
                  pl.BlockSpec(blk, lambda h, b: (b, 0, 2 * N_HEADS_DIL + h)),
                  side_spec],
        out_specs=[pl.BlockSpec(blk, lambda h, b: (b, 0, h)), side_spec],
        out_shape=[jax.ShapeDtypeStruct((batch, seq, D_DIL), jnp.float32),
                   jax.ShapeDtypeStruct(w_side.shape, jnp.bfloat16)],
        scratch_shapes=[rows] * 3 + [per_pattern] * 3
        + [pltpu.VMEM((2 * len(DILATIONS), BLOCK, 2 * BLOCK), jnp.float32)],
        compiler_params=_compiler_params(("arbitrary", "arbitrary")),
        name="dilated_attn",
    )(rel_bias, hcat3, hcat3, hcat3, w_side)


def _mix_kernel(x_ref, odil_ref, u_ref, v_ref, qm_ref, kv_ref, wsp_ref, bsp_ref, sgg_ref, sgb_ref,
                wout_ref, g1_ref, b1_ref, o_ref, ob_ref):
    tm = x_ref.shape[0]

    r_i = lax.broadcasted_iota(jnp.int32, (SG_CHUNK, SG_CHUNK), 0)
    c_i = lax.broadcasted_iota(jnp.int32, (SG_CHUNK, SG_CHUNK), 1)
    causal = r_i >= c_i
    w_sp = [jnp.where(causal, wsp_ref[g], 0.0).astype(jnp.bfloat16) for g in range(N_GROUPS_SG)]
    bsp = bsp_ref[...]
    u = jax.nn.gelu(u_ref[...])
    v = _layer_norm(jax.nn.gelu(v_ref[...]), sgg_ref[...], sgb_ref[...]).astype(jnp.bfloat16)
    sg_rows = []
    for ci in range(tm // SG_CHUNK):
        rs = slice(ci * SG_CHUNK, (ci + 1) * SG_CHUNK)
        cols = []
        for g in range(N_GROUPS_SG):
            cs = slice(g * SG_CHUNK, (g + 1) * SG_CHUNK)
            mixed = jnp.dot(w_sp[g], v[rs, cs], preferred_element_type=jnp.float32)
            cols.append((u[rs, cs] * (mixed + bsp[:, g:g + 1])).astype(jnp.bfloat16))
        sg_rows.append(jnp.concatenate(cols, axis=1))
    o_sg = jnp.concatenate(sg_rows, axis=0)

    qm = qm_ref[...].astype(jnp.bfloat16)
    mem_cols = []
    for hh in range(N_HEADS_MEM):
        cs = slice(hh * HEAD_DIM, (hh + 1) * HEAD_DIM)
        kh = kv_ref[:, hh * HEAD_DIM:(hh + 1) * HEAD_DIM]
        vh = kv_ref[:, D_MEM_ATTN + hh * HEAD_DIM:D_MEM_ATTN + (hh + 1) * HEAD_DIM]
        s = lax.dot_general(qm[:, cs], kh, _NT_DIMS,
                            preferred_element_type=jnp.float32) * ATTN_SCALE
        s_max = jnp.max(s, axis=-1, keepdims=True)
        e = jnp.exp(s - s_max)
        den = jnp.sum(e, axis=-1, keepdims=True)
        o = jnp.dot(e.astype(jnp.bfloat16), vh, preferred_element_type=jnp.float32) / den
        mem_cols.append(o.astype(jnp.bfloat16))
    mix_in = jnp.concatenate([odil_ref[...].astype(jnp.bfloat16), o_sg] + mem_cols, axis=1)
    mix = jnp.dot(mix_in, wout_ref[...], preferred_element_type=jnp.float32)
    x1 = _layer_norm(DEEPNORM_ALPHA * x_ref[...] + mix, g1_ref[...], b1_ref[...])
    o_ref[...] = x1
    ob_ref[...] = x1.astype(jnp.bfloat16)


def _mix(x2d, o_dil2d, hcat, kv, w_sp, b_sp_t, sg_g, sg_b, w_out, g1, b1, seq, tm):
    m, d_model = x2d.shape
    tiles_per_batch = seq // tm
    n_mem = kv.shape[1]
    u_blk = 3 * D_DIL // D_SG
    const = lambda i: (0, 0)
    return pl.pallas_call(
        _mix_kernel,
        grid=(m // tm,),
        in_specs=[pl.BlockSpec((tm, d_model), lambda i: (i, 0)),
                  pl.BlockSpec((tm, D_DIL), lambda i: (i, 0)),
                  pl.BlockSpec((tm, D_SG), lambda i: (i, u_blk)),
                  pl.BlockSpec((tm, D_SG), lambda i: (i, u_blk + 1)),
                  pl.BlockSpec((tm, D_MEM_ATTN), lambda i: (i, u_blk + 2)),
                  pl.BlockSpec((None, n_mem, 2 * D_MEM_ATTN), lambda i: (i // tiles_per_batch, 0, 0)),
                  _resident((N_GROUPS_SG, SG_CHUNK, SG_CHUNK), lambda i: (0, 0, 0)),
                  _resident((SG_CHUNK, N_GROUPS_SG), const),
                  _resident((1, D_SG), const),
                  _resident((1, D_SG), const),
                  _resident((d_model, d_model), const),
                  _resident((1, d_model), const),
                  _resident((1, d_model), const)],
        out_specs=[pl.BlockSpec((tm, d_model), lambda i: (i, 0)),
                   pl.BlockSpec((tm, d_model), lambda i: (i, 0))],
        out_shape=[jax.ShapeDtypeStruct((m, d_model), jnp.float32),
                   jax.ShapeDtypeStruct((m, d_model), jnp.bfloat16)],
        compiler_params=_compiler_params(("parallel",)),
        name="mix_ln1",
    )(x2d, o_dil2d, hcat, hcat, hcat, kv, w_sp, b_sp_t, sg_g, sg_b, w_out, g1, b1)


FFN_COL_SPLIT = 2


def _ffn_hidden_kernel(x_ref, wg_ref, wu_ref, side_ref, h_ref, side_out_ref, wgb_ref, wub_ref):
    @pl.when(pl.program_id(1) == 0)
    def _():
        wgb_ref[...] = wg_ref[...].astype(jnp.bfloat16)
        wub_ref[...] = wu_ref[...].astype(jnp.bfloat16)

    side_out_ref[...] = side_ref[...].astype(jnp.bfloat16)
    x = x_ref[...]
    bn = h_ref.shape[1]
    part = bn // FFN_COL_SPLIT
    for k in range(FFN_COL_SPLIT):
        cs = slice(k * part, (k + 1) * part)
        gate = jnp.dot(x, wgb_ref[:, cs], preferred_element_type=jnp.float32)
        up = jnp.dot(x, wub_ref[:, cs], preferred_element_type=jnp.float32)
        h_ref[:, cs] = (jax.nn.silu(gate) * up).astype(h_ref.dtype)


def _ffn_hidden(x1b, wg, wu, w_side, bm, bn):
    m, d_model = x1b.shape
    d_ff = wg.shape[1]
    grid = (d_ff // bn, m // bm)
    side_rows = w_side.shape[0] // (grid[0] * grid[1])
    side_spec = pl.BlockSpec((side_rows, w_side.shape[1]), lambda j, i: (j * grid[1] + i, 0))
    return pl.pallas_call(
        _ffn_hidden_kernel,
        grid=grid,
        in_specs=[pl.BlockSpec((bm, d_model), lambda j, i: (i, 0)),
                  pl.BlockSpec((d_model, bn), lambda j, i: (0, j)),
                  pl.BlockSpec((d_model, bn), lambda j, i: (0, j)),
                  side_spec],
        out_specs=[pl.BlockSpec((bm, bn), lambda j, i: (i, j)), side_spec],
        out_shape=[jax.ShapeDtypeStruct((m, d_ff), jnp.bfloat16),
                   jax.ShapeDtypeStruct(w_side.shape, jnp.bfloat16)],
        scratch_shapes=[pltpu.VMEM((d_model, bn), jnp.bfloat16),
                        pltpu.VMEM((d_model, bn), jnp.bfloat16)],
        compiler_params=_compiler_params(("parallel", "arbitrary")),
        name="ffn_hidden",
    )(x1b, wg, wu, w_side)


def _ffn_down_kernel(h_ref, x_ref, wd_ref, g2_ref, b2_ref, o_ref, f_ref):
    sub = h_ref.shape[0] // ROW_SPLIT
    row_parts = [slice(k * sub, (k + 1) * sub) for k in range(ROW_SPLIT)]
    for rows in row_parts:
        f_ref[rows, :] = jnp.dot(h_ref[rows, :], wd_ref[...], preferred_element_type=jnp.float32)
        o_ref[rows, :] = _layer_norm(DEEPNORM_ALPHA * x_ref[rows, :] + f_ref[rows, :],
                                     g2_ref[...], b2_ref[...])


def _ffn_down(hid, x1, wd, g2, b2, tm):
    m, d_ff = hid.shape
    d_model = wd.shape[1]
    const = lambda i: (0, 0)
    return pl.pallas_call(
        _ffn_down_kernel,
        grid=(m // tm,),
        in_specs=[pl.BlockSpec((tm, d_ff), lambda i: (i, 0)),
                  pl.BlockSpec((tm, d_model), lambda i: (i, 0)),
                  _resident((d_ff, d_model), const),
                  _resident((1, d_model), const),
                  _resident((1, d_model), const)],
        out_specs=pl.BlockSpec((tm, d_model), lambda i: (i, 0)),
        out_shape=jax.ShapeDtypeStruct((m, d_model), jnp.float32),
        scratch_shapes=[pltpu.VMEM((tm, d_model), jnp.float32)],
        compiler_params=_compiler_params(("parallel",)),
        name="ffn_down_ln2",
    )(hid, x1, wd, g2, b2)


def kernel(x, mem, w_in, rel_bias, sg_ln_g, sg_ln_b, w_spatial, b_spatial, w_mem_kv, w_out,
           ln1_g, ln1_b, w_gate, w_up, w_down, ln2_g, ln2_b):
    batch, seq, d_model = x.shape
    n_mem = mem.shape[1]
    depth = w_in.shape[0]
    h2d = x.reshape(batch * seq, d_model)
    for l in range(depth):
        hcat = _proj(h2d, w_in[l].astype(jnp.bfloat16), bm=1024, bn=PROJ_BN)
        kv = _kv_proj(mem.reshape(batch * n_mem, d_model), w_mem_kv[l], bn=D_MEM_ATTN)
        o_dil, w_out_b = _dilated_attention(hcat, rel_bias, w_out[l], batch, seq)
        x1, x1b = _mix(h2d, o_dil.reshape(batch * seq, D_DIL), hcat,
                       kv.reshape(batch, n_mem, 2 * D_MEM_ATTN), w_spatial[l], b_spatial[l].T,
                       sg_ln_g[l][None], sg_ln_b[l][None], w_out_b,
                       ln1_g[l][None], ln1_b[l][None], seq, tm=512)
        hid, w_down_b = _ffn_hidden(x1b, w_gate[l], w_up[l], w_down[l], bm=2048, bn=512)
        h2d = _ffn_down(hid, x1, w_down_b, ln2_g[l][None], ln2_b[l][None], tm=512)
    return h2d.reshape(batch, seq, d_model)
```

```python
import math

import jax
import jax.numpy as jnp
import numpy as np
from jax import lax
from jax.experimental import pallas as pl
from jax.experimental.pallas import tpu as pltpu

D_MODEL = 2048
HEAD_DIM = 128
N_HEADS_DIL = 8
D_DIL = N_HEADS_DIL * HEAD_DIM
DILATIONS = (16, 4, 1)
N_STEPS = 128
BLOCK = 128
N_GROUPS_SG = 4
SG_CHUNK = 128
D_SG = N_GROUPS_SG * SG_CHUNK
N_HEADS_MEM = 4
D_MEM_ATTN = N_HEADS_MEM * HEAD_DIM
N_BUCKETS = 32
MAX_DISTANCE = 2048
DEEPNORM_ALPHA = 2.0 ** 0.25
LN_EPS = 1e-5
ATTN_SCALE = HEAD_DIM ** -0.5
LOG2_E = math.log2(math.e)
MASK_VALUE = float("-inf")
N_CLASSES = 16
GROUP = 4
ATTN_UNROLL = 16
COMBINE_ROWS = 64
PROJ_BN = 1536
ROW_SPLIT = 2

VMEM_LIMIT_BYTES = 58 * 1024 * 1024

_NT_DIMS = (((1,), (1,)), ((), ()))


def _layer_norm(y, g, b):
    mu = jnp.mean(y, axis=-1, keepdims=True)
    d = y - mu
    var = jnp.mean(d * d, axis=-1, keepdims=True)
    return d * lax.rsqrt(var + LN_EPS) * g + b


def _compiler_params(semantics):
    return pltpu.CompilerParams(dimension_semantics=semantics, vmem_limit_bytes=VMEM_LIMIT_BYTES)


def _resident(block_shape, index_map):
    return pl.BlockSpec(block_shape, index_map, pipeline_mode=pl.Buffered(1))


def _proj_kernel(x_ref, w_ref, o_ref, xb_ref):
    bn = o_ref.shape[1]
    for jj in range(w_ref.shape[1] // bn):
        @pl.when(pl.program_id(1) == jj)
        def _(jj=jj):
            if jj == 0:
                xb_ref[...] = x_ref[...].astype(jnp.bfloat16)
            o_ref[...] = jnp.dot(xb_ref[...], w_ref[:, jj * bn:(jj + 1) * bn],
                                 preferred_element_type=jnp.float32)


def _proj(x2d, w_bf16, bm, bn):
    m, k = x2d.shape
    n = w_bf16.shape[1]
    return pl.pallas_call(
        _proj_kernel,
        grid=(m // bm, n // bn),
        in_specs=[pl.BlockSpec((bm, k), lambda i, j: (i, 0)),
                  _resident((k, n), lambda i, j: (0, 0))],
        out_specs=pl.BlockSpec((bm, bn), lambda i, j: (i, j)),
        out_shape=jax.ShapeDtypeStruct((m, n), jnp.float32),
        scratch_shapes=[pltpu.VMEM((bm, k), jnp.bfloat16)],
        compiler_params=_compiler_params(("parallel", "arbitrary")),
        name="proj",
    )(x2d, w_bf16)


def _kv_kernel(x_ref, w_ref, o_ref):
    o_ref[...] = jnp.dot(x_ref[...].astype(jnp.bfloat16), w_ref[...].astype(jnp.bfloat16),
                         preferred_element_type=jnp.float32).astype(o_ref.dtype)


def _kv_proj(mem2d, w, bn):
    m, k = mem2d.shape
    n = w.shape[1]
    return pl.pallas_call(
        _kv_kernel,
        grid=(n // bn,),
        in_specs=[pl.BlockSpec((m, k), lambda j: (0, 0)), pl.BlockSpec((k, bn), lambda j: (0, j))],
        out_specs=pl.BlockSpec((m, bn), lambda j: (0, j)),
        out_shape=jax.ShapeDtypeStruct((m, n), jnp.bfloat16),
        compiler_params=_compiler_params(("parallel",)),
        name="kv_proj",
    )(mem2d, w)


def _t5_bucket_starts():
    max_exact = N_BUCKETS // 2
    dist = np.arange(MAX_DISTANCE + 1)
    d = np.maximum(dist, 1).astype(np.float32)
    large = max_exact + (np.log(d / np.float32(max_exact)) / np.float32(math.log(MAX_DISTANCE / max_exact))
                         * np.float32(N_BUCKETS - max_exact)).astype(np.int32)
    bucket = np.where(dist < max_exact, dist, np.minimum(large, N_BUCKETS - 1))
    assert (np.diff(bucket) >= 0).all() and bucket[-1] == N_BUCKETS - 1
    return tuple(int(np.argmax(bucket >= b)) for b in range(N_BUCKETS))


_T5_BUCKET_STARTS = _t5_bucket_starts()


def _rows(start, rows, stride):
    return pl.ds(start, rows) if stride == 1 else pl.ds(start, rows, stride=stride)


def _block_plan(dil, blk, quarter):
    if dil == N_CLASSES:
        c, n = divmod(blk, quarter * GROUP // N_CLASSES // BLOCK)
        base, stride = (c % GROUP) * quarter + c // GROUP, GROUP
    elif dil == GROUP:
        g, n = divmod(blk, quarter // BLOCK)
        base, stride = g * quarter, 1
    else:
        assert dil == 1
        q_rows, first = BLOCK // GROUP, int(blk == 0)
        starts = [g * quarter + blk * q_rows for g in range(GROUP)]
        return ([(s, q_rows, 1) for s in starts],
                [(s - (1 - first) * q_rows, 2 * q_rows, 1) for s in starts], first)
    first = int(n == 0)
    q_runs = [(base + stride * BLOCK * n, BLOCK, stride)]
    k_runs = [(base, BLOCK, stride)] if first else [(base + stride * BLOCK * (n - 1), 2 * BLOCK, stride)]
    return q_runs, k_runs, first


def _dil_attn_kernel(rb_ref, q_ref, k_ref, v_ref, side_ref, o_ref, side_out_ref,
                     qs_ref, ks_ref, vs_ref, acc_ref, m_ref, l_ref, bias_ref):
    seq = q_ref.shape[0]
    quarter = seq // GROUP
    h = pl.program_id(0)

    side_out_ref[...] = side_ref[...].astype(jnp.bfloat16)

    for g in range(GROUP):
        dst = slice(g * quarter, (g + 1) * quarter)
        src = pl.ds(g, quarter, stride=GROUP)
        qs_ref[dst, :] = q_ref[src, :] * (ATTN_SCALE * LOG2_E)
        ks_ref[dst, :] = k_ref[src, :]
        vs_ref[dst, :] = v_ref[src, :]

    @pl.when(pl.program_id(1) == 0)
    def _():
        row = lax.broadcasted_iota(jnp.int32, (BLOCK, 2 * BLOCK), 0)
        col = lax.broadcasted_iota(jnp.int32, (BLOCK, 2 * BLOCK), 1)
        for p, dil in enumerate(DILATIONS):
            n_cls = GROUP if dil == 1 else 1
            q_rows = BLOCK // n_cls
            k_rows = 2 * BLOCK // n_cls
            q_sub = (row % q_rows) * n_cls + row // q_rows
            k_sub = (col % k_rows) * n_cls + col // k_rows
            for variant, shift in enumerate((N_STEPS, 0)):
                steps = q_sub + shift - k_sub
                valid = (steps >= 0) & (steps <= N_STEPS)
                dist = jnp.maximum(steps, 0) * dil
                bias = jnp.full((BLOCK, 2 * BLOCK), rb_ref[0, h], jnp.float32)
                for b in range(1, N_BUCKETS):
                    if _T5_BUCKET_STARTS[b] <= N_STEPS * dil:
                        bias = jnp.where(dist >= _T5_BUCKET_STARTS[b], rb_ref[b, h], bias)
                bias_ref[2 * p + variant] = jnp.where(valid, bias * LOG2_E, MASK_VALUE)

    def attend(p, plans):
        staged = []
        for q_runs, k_runs, first in plans:
            qb = jnp.concatenate([qs_ref[_rows(*run), :] for run in q_runs], axis=0)
            kb = jnp.concatenate([ks_ref[_rows(*run), :] for run in k_runs], axis=0)
            s = lax.dot_general(qb.astype(jnp.bfloat16), kb.astype(jnp.bfloat16), _NT_DIMS,
                                preferred_element_type=jnp.float32)
            n_keys = sum(run[1] for run in k_runs)
            staged.append((q_runs, k_runs, s + bias_ref[2 * p + first, :, :n_keys]))
        softmaxed = []
        for q_runs, k_runs, s in staged:
            m_blk = jnp.max(s, axis=-1, keepdims=True)
            e = jnp.exp2(s - m_blk)
            l_blk = jnp.sum(e, axis=-1, keepdims=True)
            softmaxed.append((q_runs, k_runs, m_blk, l_blk, e.astype(jnp.bfloat16)))
        for q_runs, k_runs, m_blk, l_blk, e in softmaxed:
            vb = jnp.concatenate([vs_ref[_rows(*run), :] for run in k_runs], axis=0)
            pv = jnp.dot(e, vb.astype(jnp.bfloat16), preferred_element_type=jnp.float32)
            done = 0
            for run in q_runs:
                rows, sl = _rows(*run), slice(done, done + run[1])
                m_ref[p, rows, :] = jnp.broadcast_to(m_blk[sl], (run[1], HEAD_DIM))
                l_ref[p, rows, :] = jnp.broadcast_to(l_blk[sl], (run[1], HEAD_DIM))
                acc_ref[p, rows, :] = pv[sl]
                done += run[1]

    for p, dil in enumerate(DILATIONS):
        blocks = list(range(seq // BLOCK))
        if dil == N_CLASSES:
            blocks = blocks[0::2] + blocks[1::2]
        for i in range(0, len(blocks), ATTN_UNROLL):
            attend(p, [_block_plan(dil, blk, quarter) for blk in blocks[i:i + ATTN_UNROLL]])

    for g in range(GROUP):
        for j0 in range(0, quarter, COMBINE_ROWS):
            rows = slice(g * quarter + j0, g * quarter + j0 + COMBINE_ROWS)
            maxes = [m_ref[p, rows, :] for p in range(len(DILATIONS))]
            m_all = jnp.maximum(jnp.maximum(maxes[0], maxes[1]), maxes[2])
            den = jnp.zeros((COMBINE_ROWS, HEAD_DIM), jnp.float32)
            num = jnp.zeros((COMBINE_ROWS, HEAD_DIM), jnp.float32)
            for p in range(len(DILATIONS)):
                w = jnp.exp2(maxes[p] - m_all)
                den = den + w * l_ref[p, rows, :]
                num = num + w * acc_ref[p, rows, :]
            o_ref[pl.ds(g + GROUP * j0, COMBINE_ROWS, stride=GROUP), :] = num / den


def _dilated_attention(hcat, rel_bias, w_side, batch, seq):
    hcat3 = hcat.reshape(batch, seq, hcat.shape[-1])
    blk = (None, seq, HEAD_DIM)
    rows = pltpu.VMEM((seq, HEAD_DIM), jnp.float32)
    per_pattern = pltpu.VMEM((len(DILATIONS), seq, HEAD_DIM), jnp.float32)
    side_rows = w_side.shape[0] // (N_HEADS_DIL * batch)
    side_spec = pl.BlockSpec((side_rows, w_side.shape[1]), lambda h, b: (h * batch + b, 0))
    return pl.pallas_call(
        _dil_attn_kernel,
        grid=(N_HEADS_DIL, batch),
        in_specs=[pl.BlockSpec(memory_space=pltpu.SMEM),
                  pl.BlockSpec(blk, lambda h, b: (b, 0, h)),
                  pl.BlockSpec(blk, lambda h, b: (b, 0, N_HEADS_DIL + h)),
                  pl.BlockSpec(blk, lambda h, b: (b, 0, 2 * N_HEADS_DIL + h)),
                  side_spec],
        out_specs=[pl.BlockSpec(blk, lambda h, b: (b, 0, h)), side_spec],
        out_shape=[jax.ShapeDtypeStruct((batch, seq, D_DIL), jnp.float32),
                   jax.ShapeDtypeStruct(w_side.shape, jnp.bfloat16)],
        scratch_shapes=[rows] * 3 + [per_pattern] * 3
        + [pltpu.VMEM((2 * len(DILATIONS), BLOCK, 2 * BLOCK), jnp.float32)],
        compiler_params=_compiler_params(("arbitrary", "arbitrary")),
        name="dilated_attn",
    )(rel_bias, hcat3, hcat3, hcat3, w_side)


def _mix_kernel(x_ref, odil_ref, u_ref, v_ref, qm_ref, kv_ref, wsp_ref, bsp_ref, sgg_ref, sgb_ref,
                wout_ref, g1_ref, b1_ref, o_ref, ob_ref):
    tm = x_ref.shape[0]

    r_i = lax.broadcasted_iota(jnp.int32, (SG_CHUNK, SG_CHUNK), 0)
    c_i = lax.broadcasted_iota(jnp.int32, (SG_CHUNK, SG_CHUNK), 1)
    causal = r_i >= c_i
    w_sp = [jnp.where(causal, wsp_ref[g], 0.0).astype(jnp.bfloat16) for g in range(N_GROUPS_SG)]
    bsp = bsp_ref[...]
    u = jax.nn.gelu(u_ref[...])
    v = _layer_norm(jax.nn.gelu(v_ref[...]), sgg_ref[...], sgb_ref[...]).astype(jnp.bfloat16)
    sg_rows = []
    for ci in range(tm // SG_CHUNK):
        rs = slice(ci * SG_CHUNK, (ci + 1) * SG_CHUNK)
        cols = []
        for g in range(N_GROUPS_SG):
            cs = slice(g * SG_CHUNK, (g + 1) * SG_CHUNK)
            mixed = jnp.dot(w_sp[g], v[rs, cs], preferred_element_type=jnp.float32)
            cols.append((u[rs, cs] * (mixed + bsp[:, g:g + 1])).astype(jnp.bfloat16))
        sg_rows.append(jnp.concatenate(cols, axis=1))
    o_sg = jnp.concatenate(sg_rows, axis=0)

    qm = qm_ref[...].astype(jnp.bfloat16)
    mem_cols = []
    for hh in range(N_HEADS_MEM):
        cs = slice(hh * HEAD_DIM, (hh + 1) * HEAD_DIM)
        kh = kv_ref[:, hh * HEAD_DIM:(hh + 1) * HEAD_DIM]
        vh = kv_ref[:, D_MEM_ATTN + hh * HEAD_DIM:D_MEM_ATTN + (hh + 1) * HEAD_DIM]
        s = lax.dot_general(qm[:, cs], kh, _NT_DIMS,
                            preferred_element_type=jnp.float32) * ATTN_SCALE
        s_max = jnp.max(s, axis=-1, keepdims=True)
        e = jnp.exp(s - s_max)
        den = jnp.sum(e, axis=-1, keepdims=True)
        o = jnp.dot(e.astype(jnp.bfloat16), vh, preferred_element_type=jnp.float32) / den
        mem_cols.append(o.astype(jnp.bfloat16))
    mix_in = jnp.concatenate([odil_ref[...].astype(jnp.bfloat16), o_sg] + mem_cols, axis=1)
    mix = jnp.dot(mix_in, wout_ref[...], preferred_element_type=jnp.float32)
    x1 = _layer_norm(DEEPNORM_ALPHA * x_ref[...] + mix, g1_ref[...], b1_ref[...])
    o_ref[...] = x1
    ob_ref[...] = x1.astype(jnp.bfloat16)


def _mix(x2d, o_dil2d, hcat, kv, w_sp, b_sp_t, sg_g, sg_b, w_out, g1, b1, seq, tm):
    m, d_model = x2d.shape
    tiles_per_batch = seq // tm
    n_mem = kv.shape[1]
    u_blk = 3 * D_DIL // D_SG
    const = lambda i: (0, 0)
    return pl.pallas_call(
        _mix_kernel,
        grid=(m // tm,),
        in_specs=[pl.BlockSpec((tm, d_model), lambda i: (i, 0)),
                  pl.BlockSpec((tm, D_DIL), lambda i: (i, 0)),
                  pl.BlockSpec((tm, D_SG), lambda i: (i, u_blk)),
                  pl.BlockSpec((tm, D_SG), lambda i: (i, u_blk + 1)),
                  pl.BlockSpec((tm, D_MEM_ATTN), lambda i: (i, u_blk + 2)),
                  pl.BlockSpec((None, n_mem, 2 * D_MEM_ATTN), lambda i: (i // tiles_per_batch, 0, 0)),
                  _resident((N_GROUPS_SG, SG_CHUNK, SG_CHUNK), lambda i: (0, 0, 0)),
                  _resident((SG_CHUNK, N_GROUPS_SG), const),
                  _resident((1, D_SG), const),
                  _resident((1, D_SG), const),
                  _resident((d_model, d_model), const),
                  _resident((1, d_model), const),
                  _resident((1, d_model), const)],
        out_specs=[pl.BlockSpec((tm, d_model), lambda i: (i, 0)),
                   pl.BlockSpec((tm, d_model), lambda i: (i, 0))],
        out_shape=[jax.ShapeDtypeStruct((m, d_model), jnp.float32),
                   jax.ShapeDtypeStruct((m, d_model), jnp.bfloat16)],
        compiler_params=_compiler_params(("parallel",)),
        name="mix_ln1",
    )(x2d, o_dil2d, hcat, hcat, hcat, kv, w_sp, b_sp_t, sg_g, sg_b, w_out, g1, b1)


FFN_COL_SPLIT = 2
FFN_ROW_PART = 1024


def _ffn_hidden_kernel(x_ref, wg_ref, wu_ref, side_ref, h_ref, side_out_ref, wgb_ref, wub_ref):
    @pl.when(pl.program_id(1) == 0)
    def _():
        wgb_ref[...] = wg_ref[...].astype(jnp.bfloat16)
        wub_ref[...] = wu_ref[...].astype(jnp.bfloat16)

    side_out_ref[...] = side_ref[...].astype(jnp.bfloat16)
    bm, bn = h_ref.shape
    part = bn // FFN_COL_SPLIT
    for r0 in range(0, bm, FFN_ROW_PART):
        rs = slice(r0, r0 + FFN_ROW_PART)
        x = x_ref[rs, :]
        for k in range(FFN_COL_SPLIT):
            cs = slice(k * part, (k + 1) * part)
            gate = jnp.dot(x, wgb_ref[:, cs], preferred_element_type=jnp.float32)
            up = jnp.dot(x, wub_ref[:, cs], preferred_element_type=jnp.float32)
            h_ref[rs, cs] = (jax.nn.silu(gate) * up).astype(h_ref.dtype)


def _ffn_hidden(x1b, wg, wu, w_side, bm, bn):
    m, d_model = x1b.shape
    d_ff = wg.shape[1]
    grid = (d_ff // bn, m // bm)
    side_rows = w_side.shape[0] // (grid[0] * grid[1])
    side_spec = pl.BlockSpec((side_rows, w_side.shape[1]), lambda j, i: (j * grid[1] + i, 0))
    return pl.pallas_call(
        _ffn_hidden_kernel,
        grid=grid,
        in_specs=[pl.BlockSpec((bm, d_model), lambda j, i: (i, 0)),
                  pl.BlockSpec((d_model, bn), lambda j, i: (0, j)),
                  pl.BlockSpec((d_model, bn), lambda j, i: (0, j)),
                  side_spec],
        out_specs=[pl.BlockSpec((bm, bn), lambda j, i: (i, j)), side_spec],
        out_shape=[jax.ShapeDtypeStruct((m, d_ff), jnp.bfloat16),
                   jax.ShapeDtypeStruct(w_side.shape, jnp.bfloat16)],
        scratch_shapes=[pltpu.VMEM((d_model, bn), jnp.bfloat16),
                        pltpu.VMEM((d_model, bn), jnp.bfloat16)],
        compiler_params=_compiler_params(("parallel", "arbitrary")),
        name="ffn_hidden",
    )(x1b, wg, wu, w_side)


def _ffn_down_kernel(h_ref, x_ref, wd_ref, g2_ref, b2_ref, o_ref, f_ref):
    sub = h_ref.shape[0] // ROW_SPLIT
    row_parts = [slice(k * sub, (k + 1) * sub) for k in range(ROW_SPLIT)]
    for rows in row_parts:
        f_ref[rows, :] = jnp.dot(h_ref[rows, :], wd_ref[...], preferred_element_type=jnp.float32)
        o_ref[rows, :] = _layer_norm(DEEPNORM_ALPHA * x_ref[rows, :] + f_ref[rows, :],
                                     g2_ref[...], b2_ref[...])


def _ffn_down(hid, x1, wd, g2, b2, tm):
    m, d_ff = hid.shape
    d_model = wd.shape[1]
    const = lambda i: (0, 0)
    return pl.pallas_call(
        _ffn_down_kernel,
        grid=(m // tm,),
        in_specs=[pl.BlockSpec((tm, d_ff), lambda i: (i, 0)),
                  pl.BlockSpec((tm, d_model), lambda i: (i, 0)),
                  _resident((d_ff, d_model), const),
                  _resident((1, d_model), const),
                  _resident((1, d_model), const)],
        out_specs=pl.BlockSpec((tm, d_model), lambda i: (i, 0)),
        out_shape=jax.ShapeDtypeStruct((m, d_model), jnp.float32),
        scratch_shapes=[pltpu.VMEM((tm, d_model), jnp.float32)],
        compiler_params=_compiler_params(("parallel",)),
        name="ffn_down_ln2",
    )(hid, x1, wd, g2, b2)


def kernel(x, mem, w_in, rel_bias, sg_ln_g, sg_ln_b, w_spatial, b_spatial, w_mem_kv, w_out,
           ln1_g, ln1_b, w_gate, w_up, w_down, ln2_g, ln2_b):
    batch, seq, d_model = x.shape
    n_mem = mem.shape[1]
    depth = w_in.shape[0]
    h2d = x.reshape(batch * seq, d_model)
    for l in range(depth):
        hcat = _proj(h2d, w_in[l].astype(jnp.bfloat16), bm=1024, bn=PROJ_BN)
        kv = _kv_proj(mem.reshape(batch * n_mem, d_model), w_mem_kv[l], bn=D_MEM_ATTN)
        o_dil, w_out_b = _dilated_attention(hcat, rel_bias, w_out[l], batch, seq)
        x1, x1b = _mix(h2d, o_dil.reshape(batch * seq, D_DIL), hcat,
                       kv.reshape(batch, n_mem, 2 * D_MEM_ATTN), w_spatial[l], b_spatial[l].T,
                       sg_ln_g[l][None], sg_ln_b[l][None], w_out_b,
                       ln1_g[l][None], ln1_b[l][None], seq, tm=512)
        hid, w_down_b = _ffn_hidden(x1b, w_gate[l], w_up[l], w_down[l], bm=2048, bn=512)
        h2d = _ffn_down(hid, x1, w_down_b, ln2_g[l][None], ln2_b[l][None], tm=512)
    return h2d.reshape(batch, seq, d_model)
```

```python
import math

import jax
import jax.numpy as jnp
import numpy as np
from jax import lax
from jax.experimental import pallas as pl
from jax.experimental.pallas import tpu as pltpu

D_MODEL = 2048
HEAD_DIM = 128
N_HEADS_DIL = 8
D_DIL = N_HEADS_DIL * HEAD_DIM
DILATIONS = (16, 4, 1)
N_STEPS = 128
BLOCK = 128
N_GROUPS_SG = 4
SG_CHUNK = 128
D_SG = N_GROUPS_SG * SG_CHUNK
N_HEADS_MEM = 4
D_MEM_ATTN = N_HEADS_MEM * HEAD_DIM
N_BUCKETS = 32
MAX_DISTANCE = 2048
DEEPNORM_ALPHA = 2.0 ** 0.25
LN_EPS = 1e-5
ATTN_SCALE = HEAD_DIM ** -0.5
LOG2_E = math.log2(math.e)
MASK_VALUE = float("-inf")
N_CLASSES = 16
GROUP = 4
ATTN_UNROLL = 16
COMBINE_ROWS = 64
PROJ_BN = 1536
MIX_ROW_PART = 256
ROW_SPLIT = 2

VMEM_LIMIT_BYTES = 58 * 1024 * 1024

_NT_DIMS = (((1,), (1,)), ((), ()))


def _layer_norm(y, g, b):
    mu = jnp.mean(y, axis=-1, keepdims=True)
    d = y - mu
    var = jnp.mean(d * d, axis=-1, keepdims=True)
    return d * lax.rsqrt(var + LN_EPS) * g + b


def _compiler_params(semantics):
    return pltpu.CompilerParams(dimension_semantics=semantics, vmem_limit_bytes=VMEM_LIMIT_BYTES)


def _resident(block_shape, index_map):
    return pl.BlockSpec(block_shape, index_map, pipeline_mode=pl.Buffered(1))


def _proj_kernel(x_ref, w_ref, o_ref, xb_ref):
    bn = o_ref.shape[1]
    for jj in range(w_ref.shape[1] // bn):
        @pl.when(pl.program_id(1) == jj)
        def _(jj=jj):
            if jj == 0:
                xb_ref[...] = x_ref[...].astype(jnp.bfloat16)
            o_ref[...] = jnp.dot(xb_ref[...], w_ref[:, jj * bn:(jj + 1) * bn],
                                 preferred_element_type=jnp.float32)


def _proj(x2d, w_bf16, bm, bn):
    m, k = x2d.shape
    n = w_bf16.shape[1]
    return pl.pallas_call(
        _proj_kernel,
        grid=(m // bm, n // bn),
        in_specs=[pl.BlockSpec((bm, k), lambda i, j: (i, 0)),
                  _resident((k, n), lambda i, j: (0, 0))],
        out_specs=pl.BlockSpec((bm, bn), lambda i, j: (i, j)),
        out_shape=jax.ShapeDtypeStruct((m, n), jnp.float32),
        scratch_shapes=[pltpu.VMEM((bm, k), jnp.bfloat16)],
        compiler_params=_compiler_params(("parallel", "arbitrary")),
        name="proj",
    )(x2d, w_bf16)


def _kv_kernel(x_ref, w_ref, o_ref):
    o_ref[...] = jnp.dot(x_ref[...].astype(jnp.bfloat16), w_ref[...].astype(jnp.bfloat16),
                         preferred_element_type=jnp.float32).astype(o_ref.dtype)


def _kv_proj(mem2d, w, bn):
    m, k = mem2d.shape
    n = w.shape[1]
    return pl.pallas_call(
        _kv_kernel,
        grid=(n // bn,),
        in_specs=[pl.BlockSpec((m, k), lambda j: (0, 0)), pl.BlockSpec((k, bn), lambda j: (0, j))],
        out_specs=pl.BlockSpec((m, bn), lambda j: (0, j)),
        out_shape=jax.ShapeDtypeStruct((m, n), jnp.bfloat16),
        compiler_params=_compiler_params(("parallel",)),
        name="kv_proj",
    )(mem2d, w)


def _t5_bucket_starts():
    max_exact = N_BUCKETS // 2
    dist = np.arange(MAX_DISTANCE + 1)
    d = np.maximum(dist, 1).astype(np.float32)
    large = max_exact + (np.log(d / np.float32(max_exact)) / np.float32(math.log(MAX_DISTANCE / max_exact))
                         * np.float32(N_BUCKETS - max_exact)).astype(np.int32)
    bucket = np.where(dist < max_exact, dist, np.minimum(large, N_BUCKETS - 1))
    assert (np.diff(bucket) >= 0).all() and bucket[-1] == N_BUCKETS - 1
    return tuple(int(np.argmax(bucket >= b)) for b in range(N_BUCKETS))


_T5_BUCKET_STARTS = _t5_bucket_starts()


def _rows(start, rows, stride):
    return pl.ds(start, rows) if stride == 1 else pl.ds(start, rows, stride=stride)


def _block_plan(dil, blk, quarter):
    if dil == N_CLASSES:
        c, n = divmod(blk, quarter * GROUP // N_CLASSES // BLOCK)
        base, stride = (c % GROUP) * quarter + c // GROUP, GROUP
    elif dil == GROUP:
        g, n = divmod(blk, quarter // BLOCK)
        base, stride = g * quarter, 1
    else:
        assert dil == 1
        q_rows, first = BLOCK // GROUP, int(blk == 0)
        starts = [g * quarter + blk * q_rows for g in range(GROUP)]
        return ([(s, q_rows, 1) for s in starts],
                [(s - (1 - first) * q_rows, 2 * q_rows, 1) for s in starts], first)
    first = int(n == 0)
    q_runs = [(base + stride * BLOCK * n, BLOCK, stride)]
    k_runs = [(base, BLOCK, stride)] if first else [(base + stride * BLOCK * (n - 1), 2 * BLOCK, stride)]
    return q_runs, k_runs, first


def _dil_attn_kernel(rb_ref, q_ref, k_ref, v_ref, side_ref, o_ref, side_out_ref,
                     qs_ref, ks_ref, vs_ref, acc_ref, m_ref, l_ref, bias_ref):
    seq = q_ref.shape[0]
    quarter = seq // GROUP
    h = pl.program_id(0)

    side_out_ref[...] = side_ref[...].astype(jnp.bfloat16)

    for g in range(GROUP):
        dst = slice(g * quarter, (g + 1) * quarter)
        src = pl.ds(g, quarter, stride=GROUP)
        qs_ref[dst, :] = q_ref[src, :] * (ATTN_SCALE * LOG2_E)
        ks_ref[dst, :] = k_ref[src, :]
        vs_ref[dst, :] = v_ref[src, :]

    @pl.when(pl.program_id(1) == 0)
    def _():
        row = lax.broadcasted_iota(jnp.int32, (BLOCK, 2 * BLOCK), 0)
        col = lax.broadcasted_iota(jnp.int32, (BLOCK, 2 * BLOCK), 1)
        for p, dil in enumerate(DILATIONS):
            n_cls = GROUP if dil == 1 else 1
            q_rows = BLOCK // n_cls
            k_rows = 2 * BLOCK // n_cls
            q_sub = (row % q_rows) * n_cls + row // q_rows
            k_sub = (col % k_rows) * n_cls + col // k_rows
            for variant, shift in enumerate((N_STEPS, 0)):
                steps = q_sub + shift - k_sub
                valid = (steps >= 0) & (steps <= N_STEPS)
                dist = jnp.maximum(steps, 0) * dil
                bias = jnp.full((BLOCK, 2 * BLOCK), rb_ref[0, h], jnp.float32)
                for b in range(1, N_BUCKETS):
                    if _T5_BUCKET_STARTS[b] <= N_STEPS * dil:
                        bias = jnp.where(dist >= _T5_BUCKET_STARTS[b], rb_ref[b, h], bias)
                bias_ref[2 * p + variant] = jnp.where(valid, bias * LOG2_E, MASK_VALUE)

    def attend(p, plans):
        staged = []
        for q_runs, k_runs, first in plans:
            qb = jnp.concatenate([qs_ref[_rows(*run), :] for run in q_runs], axis=0)
            kb = jnp.concatenate([ks_ref[_rows(*run), :] for run in k_runs], axis=0)
            s = lax.dot_general(qb.astype(jnp.bfloat16), kb.astype(jnp.bfloat16), _NT_DIMS,
                                preferred_element_type=jnp.float32)
            n_keys = sum(run[1] for run in k_runs)
            staged.append((q_runs, k_runs, s + bias_ref[2 * p + first, :, :n_keys]))
        softmaxed = []
        for q_runs, k_runs, s in staged:
            m_blk = jnp.max(s, axis=-1, keepdims=True)
            e = jnp.exp2(s - m_blk)
            l_blk = jnp.sum(e, axis=-1, keepdims=True)
            softmaxed.append((q_runs, k_runs, m_blk, l_blk, e.astype(jnp.bfloat16)))
        for q_runs, k_runs, m_blk, l_blk, e in softmaxed:
            vb = jnp.concatenate([vs_ref[_rows(*run), :] for run in k_runs], axis=0)
            pv = jnp.dot(e, vb.astype(jnp.bfloat16), preferred_element_type=jnp.float32)
            done = 0
            for run in q_runs:
                rows, sl = _rows(*run), slice(done, done + run[1])
                m_ref[p, rows, :] = jnp.broadcast_to(m_blk[sl], (run[1], HEAD_DIM))
                l_ref[p, rows, :] = jnp.broadcast_to(l_blk[sl], (run[1], HEAD_DIM))
                acc_ref[p, rows, :] = pv[sl]
                done += run[1]

    for p, dil in enumerate(DILATIONS):
        blocks = list(range(seq // BLOCK))
        if dil == N_CLASSES:
            blocks = blocks[0::2] + blocks[1::2]
        for i in range(0, len(blocks), ATTN_UNROLL):
            attend(p, [_block_plan(dil, blk, quarter) for blk in blocks[i:i + ATTN_UNROLL]])

    for g in range(GROUP):
        for j0 in range(0, quarter, COMBINE_ROWS):
            rows = slice(g * quarter + j0, g * quarter + j0 + COMBINE_ROWS)
            maxes = [m_ref[p, rows, :] for p in range(len(DILATIONS))]
            m_all = jnp.maximum(jnp.maximum(maxes[0], maxes[1]), maxes[2])
            den = jnp.zeros((COMBINE_ROWS, HEAD_DIM), jnp.float32)
            num = jnp.zeros((COMBINE_ROWS, HEAD_DIM), jnp.float32)
            for p in range(len(DILATIONS)):
                w = jnp.exp2(maxes[p] - m_all)
                den = den + w * l_ref[p, rows, :]
                num = num + w * acc_ref[p, rows, :]
            o_ref[pl.ds(g + GROUP * j0, COMBINE_ROWS, stride=GROUP), :] = num / den


def _dilated_attention(hcat, rel_bias, w_side, batch, seq):
    hcat3 = hcat.reshape(batch, seq, hcat.shape[-1])
    blk = (None, seq, HEAD_DIM)
    rows = pltpu.VMEM((seq, HEAD_DIM), jnp.float32)
    per_pattern = pltpu.VMEM((len(DILATIONS), seq, HEAD_DIM), jnp.float32)
    side_rows = w_side.shape[0] // (N_HEADS_DIL * batch)
    side_spec = pl.BlockSpec((side_rows, w_side.shape[1]), lambda h, b: (h * batch + b, 0))
    return pl.pallas_call(
        _dil_attn_kernel,
        grid=(N_HEADS_DIL, batch),
        in_specs=[pl.BlockSpec(memory_space=pltpu.SMEM),
                  pl.BlockSpec(blk, lambda h, b: (b, 0, h)),
                  pl.BlockSpec(blk, lambda h, b: (b, 0, N_HEADS_DIL + h)),
                  pl.BlockSpec(blk, lambda h, b: (b, 0, 2 * N_HEADS_DIL + h)),
                  side_spec],
        out_specs=[pl.BlockSpec(blk, lambda h, b: (b, 0, h)), side_spec],
        out_shape=[jax.ShapeDtypeStruct((batch, seq, D_DIL), jnp.float32),
                   jax.ShapeDtypeStruct(w_side.shape, jnp.bfloat16)],
        scratch_shapes=[rows] * 3 + [per_pattern] * 3
        + [pltpu.VMEM((2 * len(DILATIONS), BLOCK, 2 * BLOCK), jnp.float32)],
        compiler_params=_compiler_params(("arbitrary", "arbitrary")),
        name="dilated_attn",
    )(rel_bias, hcat3, hcat3, hcat3, w_side)


def _mix_kernel(x_ref, odil_ref, u_ref, v_ref, qm_ref, kv_ref, wsp_ref, bsp_ref, sgg_ref, sgb_ref,
                wout_ref, g1_ref, b1_ref, o_ref, ob_ref):
    tm = x_ref.shape[0]

    r_i = lax.broadcasted_iota(jnp.int32, (SG_CHUNK, SG_CHUNK), 0)
    c_i = lax.broadcasted_iota(jnp.int32, (SG_CHUNK, SG_CHUNK), 1)
    causal = r_i >= c_i
    w_sp = [jnp.where(causal, wsp_ref[g], 0.0).astype(jnp.bfloat16) for g in range(N_GROUPS_SG)]
    bsp = bsp_ref[...]

    for r0 in range(0, tm, MIX_ROW_PART):
        part = slice(r0, r0 + MIX_ROW_PART)
        u = jax.nn.gelu(u_ref[part, :])
        v = _layer_norm(jax.nn.gelu(v_ref[part, :]), sgg_ref[...], sgb_ref[...]).astype(jnp.bfloat16)
        sg_rows = []
        for ci in range(MIX_ROW_PART // SG_CHUNK):
            rs = slice(ci * SG_CHUNK, (ci + 1) * SG_CHUNK)
            cols = []
            for g in range(N_GROUPS_SG):
                cs = slice(g * SG_CHUNK, (g + 1) * SG_CHUNK)
                mixed = jnp.dot(w_sp[g], v[rs, cs], preferred_element_type=jnp.float32)
                cols.append((u[rs, cs] * (mixed + bsp[:, g:g + 1])).astype(jnp.bfloat16))
            sg_rows.append(jnp.concatenate(cols, axis=1))
        o_sg = jnp.concatenate(sg_rows, axis=0)

        qm = qm_ref[part, :].astype(jnp.bfloat16)
        mem_cols = []
        for hh in range(N_HEADS_MEM):
            cs = slice(hh * HEAD_DIM, (hh + 1) * HEAD_DIM)
            kh = kv_ref[:, hh * HEAD_DIM:(hh + 1) * HEAD_DIM]
            vh = kv_ref[:, D_MEM_ATTN + hh * HEAD_DIM:D_MEM_ATTN + (hh + 1) * HEAD_DIM]
            s = lax.dot_general(qm[:, cs], kh, _NT_DIMS,
                                preferred_element_type=jnp.float32) * ATTN_SCALE
            s_max = jnp.max(s, axis=-1, keepdims=True)
            e = jnp.exp(s - s_max)
            den = jnp.sum(e, axis=-1, keepdims=True)
            o = jnp.dot(e.astype(jnp.bfloat16), vh, preferred_element_type=jnp.float32) / den
            mem_cols.append(o.astype(jnp.bfloat16))
        mix_in = jnp.concatenate([odil_ref[part, :].astype(jnp.bfloat16), o_sg] + mem_cols, axis=1)
        mix = jnp.dot(mix_in, wout_ref[...], preferred_element_type=jnp.float32)
        x1 = _layer_norm(DEEPNORM_ALPHA * x_ref[part, :] + mix, g1_ref[...], b1_ref[...])
        o_ref[part, :] = x1
        ob_ref[part, :] = x1.astype(jnp.bfloat16)


def _mix(x2d, o_dil2d, hcat, kv, w_sp, b_sp_t, sg_g, sg_b, w_out, g1, b1, seq, tm):
    m, d_model = x2d.shape
    tiles_per_batch = seq // tm
    n_mem = kv.shape[1]
    u_blk = 3 * D_DIL // D_SG
    const = lambda i: (0, 0)
    return pl.pallas_call(
        _mix_kernel,
        grid=(m // tm,),
        in_specs=[pl.BlockSpec((tm, d_model), lambda i: (i, 0)),
                  pl.BlockSpec((tm, D_DIL), lambda i: (i, 0)),
                  pl.BlockSpec((tm, D_SG), lambda i: (i, u_blk)),
                  pl.BlockSpec((tm, D_SG), lambda i: (i, u_blk + 1)),
                  pl.BlockSpec((tm, D_MEM_ATTN), lambda i: (i, u_blk + 2)),
                  pl.BlockSpec((None, n_mem, 2 * D_MEM_ATTN), lambda i: (i // tiles_per_batch, 0, 0)),
                  _resident((N_GROUPS_SG, SG_CHUNK, SG_CHUNK), lambda i: (0, 0, 0)),
                  _resident((SG_CHUNK, N_GROUPS_SG), const),
                  _resident((1, D_SG), const),
                  _resident((1, D_SG), const),
                  _resident((d_model, d_model), const),
                  _resident((1, d_model), const),
                  _resident((1, d_model), const)],
        out_specs=[pl.BlockSpec((tm, d_model), lambda i: (i, 0)),
                   pl.BlockSpec((tm, d_model), lambda i: (i, 0))],
        out_shape=[jax.ShapeDtypeStruct((m, d_model), jnp.float32),
                   jax.ShapeDtypeStruct((m, d_model), jnp.bfloat16)],
        compiler_params=_compiler_params(("parallel",)),
        name="mix_ln1",
    )(x2d, o_dil2d, hcat, hcat, hcat, kv, w_sp, b_sp_t, sg_g, sg_b, w_out, g1, b1)


FFN_COL_SPLIT = 2
FFN_ROW_PART = 1024


def _ffn_hidden_kernel(x_ref, wg_ref, wu_ref, side_ref, h_ref, side_out_ref, wgb_ref, wub_ref):
    @pl.when(pl.program_id(1) == 0)
    def _():
        wgb_ref[...] = wg_ref[...].astype(jnp.bfloat16)
        wub_ref[...] = wu_ref[...].astype(jnp.bfloat16)

    side_out_ref[...] = side_ref[...].astype(jnp.bfloat16)
    bm, bn = h_ref.shape
    part = bn // FFN_COL_SPLIT
    for r0 in range(0, bm, FFN_ROW_PART):
        rs = slice(r0, r0 + FFN_ROW_PART)
        x = x_ref[rs, :]
        for k in range(FFN_COL_SPLIT):
            cs = slice(k * part, (k + 1) * part)
            gate = jnp.dot(x, wgb_ref[:, cs], preferred_element_type=jnp.float32)
            up = jnp.dot(x, wub_ref[:, cs], preferred_element_type=jnp.float32)
            h_ref[rs, cs] = (jax.nn.silu(gate) * up).astype(h_ref.dtype)


def _ffn_hidden(x1b, wg, wu, w_side, bm, bn):
    m, d_model = x1b.shape
    d_ff = wg.shape[1]
    grid = (d_ff // bn, m // bm)
    side_rows = w_side.shape[0] // (grid[0] * grid[1])
    side_spec = pl.BlockSpec((side_rows, w_side.shape[1]), lambda j, i: (j * grid[1] + i, 0))
    return pl.pallas_call(
        _ffn_hidden_kernel,
        grid=grid,
        in_specs=[pl.BlockSpec((bm, d_model), lambda j, i: (i, 0)),
                  pl.BlockSpec((d_model, bn), lambda j, i: (0, j)),
                  pl.BlockSpec((d_model, bn), lambda j, i: (0, j)),
                  side_spec],
        out_specs=[pl.BlockSpec((bm, bn), lambda j, i: (i, j)), side_spec],
        out_shape=[jax.ShapeDtypeStruct((m, d_ff), jnp.bfloat16),
                   jax.ShapeDtypeStruct(w_side.shape, jnp.bfloat16)],
        scratch_shapes=[pltpu.VMEM((d_model, bn), jnp.bfloat16),
                        pltpu.VMEM((d_model, bn), jnp.bfloat16)],
        compiler_params=_compiler_params(("parallel", "arbitrary")),
        name="ffn_hidden",
    )(x1b, wg, wu, w_side)


def _ffn_down_kernel(h_ref, x_ref, wd_ref, g2_ref, b2_ref, o_ref, f_ref):
    sub = h_ref.shape[0] // ROW_SPLIT
    row_parts = [slice(k * sub, (k + 1) * sub) for k in range(ROW_SPLIT)]
    for rows in row_parts:
        f_ref[rows, :] = jnp.dot(h_ref[rows, :], wd_ref[...], preferred_element_type=jnp.float32)
        o_ref[rows, :] = _layer_norm(DEEPNORM_ALPHA * x_ref[rows, :] + f_ref[rows, :],
                                     g2_ref[...], b2_ref[...])


def _ffn_down(hid, x1, wd, g2, b2, tm):
    m, d_ff = hid.shape
    d_model = wd.shape[1]
    const = lambda i: (0, 0)
    return pl.pallas_call(
        _ffn_down_kernel,
        grid=(m // tm,),
        in_specs=[pl.BlockSpec((tm, d_ff), lambda i: (i, 0)),
                  pl.BlockSpec((tm, d_model), lambda i: (i, 0)),
                  _resident((d_ff, d_model), const),
                  _resident((1, d_model), const),
                  _resident((1, d_model), const)],
        out_specs=pl.BlockSpec((tm, d_model), lambda i: (i, 0)),
        out_shape=jax.ShapeDtypeStruct((m, d_model), jnp.float32),
        scratch_shapes=[pltpu.VMEM((tm, d_model), jnp.float32)],
        compiler_params=_compiler_params(("parallel",)),
        name="ffn_down_ln2",
    )(hid, x1, wd, g2, b2)


def kernel(x, mem, w_in, rel_bias, sg_ln_g, sg_ln_b, w_spatial, b_spatial, w_mem_kv, w_out,
           ln1_g, ln1_b, w_gate, w_up, w_down, ln2_g, ln2_b):
    batch, seq, d_model = x.shape
    n_mem = mem.shape[1]
    depth = w_in.shape[0]
    h2d = x.reshape(batch * seq, d_model)
    for l in range(depth):
        hcat = _proj(h2d, w_in[l].astype(jnp.bfloat16), bm=1024, bn=PROJ_BN)
        kv = _kv_proj(mem.reshape(batch * n_mem, d_model), w_mem_kv[l], bn=D_MEM_ATTN)
        o_dil, w_out_b = _dilated_attention(hcat, rel_bias, w_out[l], batch, seq)
        x1, x1b = _mix(h2d, o_dil.reshape(batch * seq, D_DIL), hcat,
                       kv.reshape(batch, n_mem, 2 * D_MEM_ATTN), w_spatial[l], b_spatial[l].T,
                       sg_ln_g[l][None], sg_ln_b[l][None], w_out_b,
                       ln1_g[l][None], ln1_b[l][None], seq, tm=512)
        hid, w_down_b = _ffn_hidden(x1b, w_gate[l], w_up[l], w_down[l], bm=2048, bn=512)
        h2d = _ffn_down(hid, x1, w_down_b, ln2_g[l][None], ln2_b[l][None], tm=512)
    return h2d.reshape(batch, seq, d_model)
```

```python
import math

import jax
import jax.numpy as jnp
import numpy as np
from jax import lax
from jax.experimental import pallas as pl
from jax.experimental.pallas import tpu as pltpu

D_MODEL = 2048
HEAD_DIM = 128
N_HEADS_DIL = 8
D_DIL = N_HEADS_DIL * HEAD_DIM
DILATIONS = (16, 4, 1)
N_STEPS = 128
BLOCK = 128
N_GROUPS_SG = 4
SG_CHUNK = 128
D_SG = N_GROUPS_SG * SG_CHUNK
N_HEADS_MEM = 4
D_MEM_ATTN = N_HEADS_MEM * HEAD_DIM
N_BUCKETS = 32
MAX_DISTANCE = 2048
DEEPNORM_ALPHA = 2.0 ** 0.25
LN_EPS = 1e-5
ATTN_SCALE = HEAD_DIM ** -0.5
LOG2_E = math.log2(math.e)
MASK_VALUE = float("-inf")
N_CLASSES = 16
GROUP = 4
ATTN_UNROLL = 16
COMBINE_ROWS = 64
PROJ_BN = 1536
ROW_SPLIT = 4

VMEM_LIMIT_BYTES = 58 * 1024 * 1024

_NT_DIMS = (((1,), (1,)), ((), ()))


def _layer_norm(y, g, b):
    mu = jnp.mean(y, axis=-1, keepdims=True)
    d = y - mu
    var = jnp.mean(d * d, axis=-1, keepdims=True)
    return d * lax.rsqrt(var + LN_EPS) * g + b


def _compiler_params(semantics):
    return pltpu.CompilerParams(dimension_semantics=semantics, vmem_limit_bytes=VMEM_LIMIT_BYTES)


def _resident(block_shape, index_map):
    return pl.BlockSpec(block_shape, index_map, pipeline_mode=pl.Buffered(1))


def _proj_kernel(x_ref, w_ref, o_ref, xb_ref):
    bn = o_ref.shape[1]
    for jj in range(w_ref.shape[1] // bn):
        @pl.when(pl.program_id(1) == jj)
        def _(jj=jj):
            if jj == 0:
                xb_ref[...] = x_ref[...].astype(jnp.bfloat16)
            o_ref[...] = jnp.dot(xb_ref[...], w_ref[:, jj * bn:(jj + 1) * bn],
                                 preferred_element_type=jnp.float32)


def _proj(x2d, w_bf16, bm, bn):
    m, k = x2d.shape
    n = w_bf16.shape[1]
    return pl.pallas_call(
        _proj_kernel,
        grid=(m // bm, n // bn),
        in_specs=[pl.BlockSpec((bm, k), lambda i, j: (i, 0)),
                  _resident((k, n), lambda i, j: (0, 0))],
        out_specs=pl.BlockSpec((bm, bn), lambda i, j: (i, j)),
        out_shape=jax.ShapeDtypeStruct((m, n), jnp.float32),
        scratch_shapes=[pltpu.VMEM((bm, k), jnp.bfloat16)],
        compiler_params=_compiler_params(("parallel", "arbitrary")),
        name="proj",
    )(x2d, w_bf16)


def _kv_kernel(x_ref, w_ref, o_ref):
    o_ref[...] = jnp.dot(x_ref[...].astype(jnp.bfloat16), w_ref[...].astype(jnp.bfloat16),
                         preferred_element_type=jnp.float32).astype(o_ref.dtype)


def _kv_proj(mem2d, w, bn):
    m, k = mem2d.shape
    n = w.shape[1]
    return pl.pallas_call(
        _kv_kernel,
        grid=(n // bn,),
        in_specs=[pl.BlockSpec((m, k), lambda j: (0, 0)), pl.BlockSpec((k, bn), lambda j: (0, j))],
        out_specs=pl.BlockSpec((m, bn), lambda j: (0, j)),
        out_shape=jax.ShapeDtypeStruct((m, n), jnp.bfloat16),
        compiler_params=_compiler_params(("parallel",)),
        name="kv_proj",
    )(mem2d, w)


def _t5_bucket_starts():
    max_exact = N_BUCKETS // 2
    dist = np.arange(MAX_DISTANCE + 1)
    d = np.maximum(dist, 1).astype(np.float32)
    large = max_exact + (np.log(d / np.float32(max_exact)) / np.float32(math.log(MAX_DISTANCE / max_exact))
                         * np.float32(N_BUCKETS - max_exact)).astype(np.int32)
    bucket = np.where(dist < max_exact, dist, np.minimum(large, N_BUCKETS - 1))
    assert (np.diff(bucket) >= 0).all() and bucket[-1] == N_BUCKETS - 1
    return tuple(int(np.argmax(bucket >= b)) for b in range(N_BUCKETS))


_T5_BUCKET_STARTS = _t5_bucket_starts()


def _rows(start, rows, stride):
    return pl.ds(start, rows) if stride == 1 else pl.ds(start, rows, stride=stride)


def _block_plan(dil, blk, quarter):
    if dil == N_CLASSES:
        c, n = divmod(blk, quarter * GROUP // N_CLASSES // BLOCK)
        base, stride = (c % GROUP) * quarter + c // GROUP, GROUP
    elif dil == GROUP:
        g, n = divmod(blk, quarter // BLOCK)
        base, stride = g * quarter, 1
    else:
        assert dil == 1
        q_rows, first = BLOCK // GROUP, int(blk == 0)
        starts = [g * quarter + blk * q_rows for g in range(GROUP)]
        return ([(s, q_rows, 1) for s in starts],
                [(s - (1 - first) * q_rows, 2 * q_rows, 1) for s in starts], first)
    first = int(n == 0)
    q_runs = [(base + stride * BLOCK * n, BLOCK, stride)]
    k_runs = [(base, BLOCK, stride)] if first else [(base + stride * BLOCK * (n - 1), 2 * BLOCK, stride)]
    return q_runs, k_runs, first


def _dil_attn_kernel(rb_ref, q_ref, k_ref, v_ref, side_ref, o_ref, side_out_ref,
                     qs_ref, ks_ref, vs_ref, acc_ref, m_ref, l_ref, bias_ref):
    seq = q_ref.shape[0]
    quarter = seq // GROUP
    h = pl.program_id(0)

    side_out_ref[...] = side_ref[...].astype(jnp.bfloat16)

    for g in range(GROUP):
        dst = slice(g * quarter, (g + 1) * quarter)
        src = pl.ds(g, quarter, stride=GROUP)
        qs_ref[dst, :] = q_ref[src, :] * (ATTN_SCALE * LOG2_E)
        ks_ref[dst, :] = k_ref[src, :]
        vs_ref[dst, :] = v_ref[src, :]

    @pl.when(pl.program_id(1) == 0)
    def _():
        row = lax.broadcasted_iota(jnp.int32, (BLOCK, 2 * BLOCK), 0)
        col = lax.broadcasted_iota(jnp.int32, (BLOCK, 2 * BLOCK), 1)
        for p, dil in enumerate(DILATIONS):
            n_cls = GROUP if dil == 1 else 1
            q_rows = BLOCK // n_cls
            k_rows = 2 * BLOCK // n_cls
            q_sub = (row % q_rows) * n_cls + row // q_rows
            k_sub = (col % k_rows) * n_cls + col // k_rows
            for variant, shift in enumerate((N_STEPS, 0)):
                steps = q_sub + shift - k_sub
                valid = (steps >= 0) & (steps <= N_STEPS)
                dist = jnp.maximum(steps, 0) * dil
                bias = jnp.full((BLOCK, 2 * BLOCK), rb_ref[0, h], jnp.float32)
                for b in range(1, N_BUCKETS):
                    if _T5_BUCKET_STARTS[b] <= N_STEPS * dil:
                        bias = jnp.where(dist >= _T5_BUCKET_STARTS[b], rb_ref[b, h], bias)
                bias_ref[2 * p + variant] = jnp.where(valid, bias * LOG2_E, MASK_VALUE)

    def attend(p, plans):
        staged = []
        for q_runs, k_runs, first in plans:
            qb = jnp.concatenate([qs_ref[_rows(*run), :] for run in q_runs], axis=0)
            kb = jnp.concatenate([ks_ref[_rows(*run), :] for run in k_runs], axis=0)
            s = lax.dot_general(qb.astype(jnp.bfloat16), kb.astype(jnp.bfloat16), _NT_DIMS,
                                preferred_element_type=jnp.float32)
            n_keys = sum(run[1] for run in k_runs)
            staged.append((q_runs, k_runs, s + bias_ref[2 * p + first, :, :n_keys]))
        softmaxed = []
        for q_runs, k_runs, s in staged:
            m_blk = jnp.max(s, axis=-1, keepdims=True)
            e = jnp.exp2(s - m_blk)
            l_blk = jnp.sum(e, axis=-1, keepdims=True)
            softmaxed.append((q_runs, k_runs, m_blk, l_blk, e.astype(jnp.bfloat16)))
        for q_runs, k_runs, m_blk, l_blk, e in softmaxed:
            vb = jnp.concatenate([vs_ref[_rows(*run), :] for run in k_runs], axis=0)
            pv = jnp.dot(e, vb.astype(jnp.bfloat16), preferred_element_type=jnp.float32)
            done = 0
            for run in q_runs:
                rows, sl = _rows(*run), slice(done, done + run[1])
                m_ref[p, rows, :] = jnp.broadcast_to(m_blk[sl], (run[1], HEAD_DIM))
                l_ref[p, rows, :] = jnp.broadcast_to(l_blk[sl], (run[1], HEAD_DIM))
                acc_ref[p, rows, :] = pv[sl]
                done += run[1]

    for p, dil in enumerate(DILATIONS):
        blocks = list(range(seq // BLOCK))
        if dil == N_CLASSES:
            blocks = blocks[0::2] + blocks[1::2]
        for i in range(0, len(blocks), ATTN_UNROLL):
            attend(p, [_block_plan(dil, blk, quarter) for blk in blocks[i:i + ATTN_UNROLL]])

    for g in range(GROUP):
        for j0 in range(0, quarter, COMBINE_ROWS):
            rows = slice(g * quarter + j0, g * quarter + j0 + COMBINE_ROWS)
            maxes = [m_ref[p, rows, :] for p in range(len(DILATIONS))]
            m_all = jnp.maximum(jnp.maximum(maxes[0], maxes[1]), maxes[2])
            den = jnp.zeros((COMBINE_ROWS, HEAD_DIM), jnp.float32)
            num = jnp.zeros((COMBINE_ROWS, HEAD_DIM), jnp.float32)
            for p in range(len(DILATIONS)):
                w = jnp.exp2(maxes[p] - m_all)
                den = den + w * l_ref[p, rows, :]
                num = num + w * acc_ref[p, rows, :]
            o_ref[pl.ds(g + GROUP * j0, COMBINE_ROWS, stride=GROUP), :] = num / den


def _dilated_attention(hcat, rel_bias, w_side, batch, seq):
    hcat3 = hcat.reshape(batch, seq, hcat.shape[-1])
    blk = (None, seq, HEAD_DIM)
    rows = pltpu.VMEM((seq, HEAD_DIM), jnp.float32)
    per_pattern = pltpu.VMEM((len(DILATIONS), seq, HEAD_DIM), jnp.float32)
    side_rows = w_side.shape[0] // (N_HEADS_DIL * batch)
    side_spec = pl.BlockSpec((side_rows, w_side.shape[1]), lambda h, b: (h * batch + b, 0))
    return pl.pallas_call(
        _dil_attn_kernel,
        grid=(N_HEADS_DIL, batch),
        in_specs=[pl.BlockSpec(memory_space=pltpu.SMEM),
                  pl.BlockSpec(blk, lambda h, b: (b, 0, h)),
                  pl.BlockSpec(blk, lambda h, b: (b, 0, N_HEADS_DIL + h)),
                  pl.BlockSpec(blk, lambda h, b: (b, 0, 2 * N_HEADS_DIL + h)),
                  side_spec],
        out_specs=[pl.BlockSpec(blk, lambda h, b: (b, 0, h)), side_spec],
        out_shape=[jax.ShapeDtypeStruct((batch, seq, D_DIL), jnp.float32),
                   jax.ShapeDtypeStruct(w_side.shape, jnp.bfloat16)],
        scratch_shapes=[rows] * 3 + [per_pattern] * 3
        + [pltpu.VMEM((2 * len(DILATIONS), BLOCK, 2 * BLOCK), jnp.float32)],
        compiler_params=_compiler_params(("arbitrary", "arbitrary")),
        name="dilated_attn",
    )(rel_bias, hcat3, hcat3, hcat3, w_side)


def _mix_kernel(x_ref, odil_ref, u_ref, v_ref, qm_ref, kv_ref, wsp_ref, bsp_ref, sgg_ref, sgb_ref,
                wout_ref, g1_ref, b1_ref, o_ref, ob_ref):
    tm = x_ref.shape[0]

    r_i = lax.broadcasted_iota(jnp.int32, (SG_CHUNK, SG_CHUNK), 0)
    c_i = lax.broadcasted_iota(jnp.int32, (SG_CHUNK, SG_CHUNK), 1)
    causal = r_i >= c_i
    w_sp = [jnp.where(causal, wsp_ref[g], 0.0).astype(jnp.bfloat16) for g in range(N_GROUPS_SG)]
    bsp = bsp_ref[...]
    u = jax.nn.gelu(u_ref[...])
    v = _layer_norm(jax.nn.gelu(v_ref[...]), sgg_ref[...], sgb_ref[...]).astype(jnp.bfloat16)
    sg_rows = []
    for ci in range(tm // SG_CHUNK):
        rs = slice(ci * SG_CHUNK, (ci + 1) * SG_CHUNK)
        cols = []
        for g in range(N_GROUPS_SG):
            cs = slice(g * SG_CHUNK, (g + 1) * SG_CHUNK)
            mixed = jnp.dot(w_sp[g], v[rs, cs], preferred_element_type=jnp.float32)
            cols.append((u[rs, cs] * (mixed + bsp[:, g:g + 1])).astype(jnp.bfloat16))
        sg_rows.append(jnp.concatenate(cols, axis=1))
    o_sg = jnp.concatenate(sg_rows, axis=0)

    qm = qm_ref[...].astype(jnp.bfloat16)
    mem_cols = []
    for hh in range(N_HEADS_MEM):
        cs = slice(hh * HEAD_DIM, (hh + 1) * HEAD_DIM)
        kh = kv_ref[:, hh * HEAD_DIM:(hh + 1) * HEAD_DIM]
        vh = kv_ref[:, D_MEM_ATTN + hh * HEAD_DIM:D_MEM_ATTN + (hh + 1) * HEAD_DIM]
        s = lax.dot_general(qm[:, cs], kh, _NT_DIMS,
                            preferred_element_type=jnp.float32) * ATTN_SCALE
        s_max = jnp.max(s, axis=-1, keepdims=True)
        e = jnp.exp(s - s_max)
        den = jnp.sum(e, axis=-1, keepdims=True)
        o = jnp.dot(e.astype(jnp.bfloat16), vh, preferred_element_type=jnp.float32) / den
        mem_cols.append(o.astype(jnp.bfloat16))
    mix_in = jnp.concatenate([odil_ref[...].astype(jnp.bfloat16), o_sg] + mem_cols, axis=1)
    mix = jnp.dot(mix_in, wout_ref[...], preferred_element_type=jnp.float32)
    x1 = _layer_norm(DEEPNORM_ALPHA * x_ref[...] + mix, g1_ref[...], b1_ref[...])
    o_ref[...] = x1
    ob_ref[...] = x1.astype(jnp.bfloat16)


def _mix(x2d, o_dil2d, hcat, kv, w_sp, b_sp_t, sg_g, sg_b, w_out, g1, b1, seq, tm):
    m, d_model = x2d.shape
    tiles_per_batch = seq // tm
    n_mem = kv.shape[1]
    u_blk = 3 * D_DIL // D_SG
    const = lambda i: (0, 0)
    return pl.pallas_call(
        _mix_kernel,
        grid=(m // tm,),
        in_specs=[pl.BlockSpec((tm, d_model), lambda i: (i, 0)),
                  pl.BlockSpec((tm, D_DIL), lambda i: (i, 0)),
                  pl.BlockSpec((tm, D_SG), lambda i: (i, u_blk)),
                  pl.BlockSpec((tm, D_SG), lambda i: (i, u_blk + 1)),
                  pl.BlockSpec((tm, D_MEM_ATTN), lambda i: (i, u_blk + 2)),
                  pl.BlockSpec((None, n_mem, 2 * D_MEM_ATTN), lambda i: (i // tiles_per_batch, 0, 0)),
                  _resident((N_GROUPS_SG, SG_CHUNK, SG_CHUNK), lambda i: (0, 0, 0)),
                  _resident((SG_CHUNK, N_GROUPS_SG), const),
                  _resident((1, D_SG), const),
                  _resident((1, D_SG), const),
                  _resident((d_model, d_model), const),
                  _resident((1, d_model), const),
                  _resident((1, d_model), const)],
        out_specs=[pl.BlockSpec((tm, d_model), lambda i: (i, 0)),
                   pl.BlockSpec((tm, d_model), lambda i: (i, 0))],
        out_shape=[jax.ShapeDtypeStruct((m, d_model), jnp.float32),
                   jax.ShapeDtypeStruct((m, d_model), jnp.bfloat16)],
        compiler_params=_compiler_params(("parallel",)),
        name="mix_ln1",
    )(x2d, o_dil2d, hcat, hcat, hcat, kv, w_sp, b_sp_t, sg_g, sg_b, w_out, g1, b1)


FFN_COL_SPLIT = 2
FFN_ROW_PART = 1024


def _ffn_hidden_kernel(x_ref, wg_ref, wu_ref, side_ref, h_ref, side_out_ref, wgb_ref, wub_ref):
    @pl.when(pl.program_id(1) == 0)
    def _():
        wgb_ref[...] = wg_ref[...].astype(jnp.bfloat16)
        wub_ref[...] = wu_ref[...].astype(jnp.bfloat16)

    side_out_ref[...] = side_ref[...].astype(jnp.bfloat16)
    bm, bn = h_ref.shape
    part = bn // FFN_COL_SPLIT
    for r0 in range(0, bm, FFN_ROW_PART):
        rs = slice(r0, r0 + FFN_ROW_PART)
        x = x_ref[rs, :]
        for k in range(FFN_COL_SPLIT):
            cs = slice(k * part, (k + 1) * part)
            gate = jnp.dot(x, wgb_ref[:, cs], preferred_element_type=jnp.float32)
            up = jnp.dot(x, wub_ref[:, cs], preferred_element_type=jnp.float32)
            h_ref[rs, cs] = (jax.nn.silu(gate) * up).astype(h_ref.dtype)


def _ffn_hidden(x1b, wg, wu, w_side, bm, bn):
    m, d_model = x1b.shape
    d_ff = wg.shape[1]
    grid = (d_ff // bn, m // bm)
    side_rows = w_side.shape[0] // (grid[0] * grid[1])
    side_spec = pl.BlockSpec((side_rows, w_side.shape[1]), lambda j, i: (j * grid[1] + i, 0))
    return pl.pallas_call(
        _ffn_hidden_kernel,
        grid=grid,
        in_specs=[pl.BlockSpec((bm, d_model), lambda j, i: (i, 0)),
                  pl.BlockSpec((d_model, bn), lambda j, i: (0, j)),
                  pl.BlockSpec((d_model, bn), lambda j, i: (0, j)),
                  side_spec],
        out_specs=[pl.BlockSpec((bm, bn), lambda j, i: (i, j)), side_spec],
        out_shape=[jax.ShapeDtypeStruct((m, d_ff), jnp.bfloat16),
                   jax.ShapeDtypeStruct(w_side.shape, jnp.bfloat16)],
        scratch_shapes=[pltpu.VMEM((d_model, bn), jnp.bfloat16),
                        pltpu.VMEM((d_model, bn), jnp.bfloat16)],
        compiler_params=_compiler_params(("parallel", "arbitrary")),
        name="ffn_hidden",
    )(x1b, wg, wu, w_side)


def _ffn_down_kernel(h_ref, x_ref, wd_ref, g2_ref, b2_ref, o_ref, f_ref):
    sub = h_ref.shape[0] // ROW_SPLIT
    row_parts = [slice(k * sub, (k + 1) * sub) for k in range(ROW_SPLIT)]
    for rows in row_parts:
        f_ref[rows, :] = jnp.dot(h_ref[rows, :], wd_ref[...], preferred_element_type=jnp.float32)
        o_ref[rows, :] = _layer_norm(DEEPNORM_ALPHA * x_ref[rows, :] + f_ref[rows, :],
                                     g2_ref[...], b2_ref[...])


def _ffn_down(hid, x1, wd, g2, b2, tm):
    m, d_ff = hid.shape
    d_model = wd.shape[1]
    const = lambda i: (0, 0)
    return pl.pallas_call(
        _ffn_down_kernel,
        grid=(m // tm,),
        in_specs=[pl.BlockSpec((tm, d_ff), lambda i: (i, 0)),
                  pl.BlockSpec((tm, d_model), lambda i: (i, 0)),
                  _resident((d_ff, d_model), const),
                  _resident((1, d_model), const),
                  _resident((1, d_model), const)],
        out_specs=pl.BlockSpec((tm, d_model), lambda i: (i, 0)),
        out_shape=jax.ShapeDtypeStruct((m, d_model), jnp.float32),
        scratch_shapes=[pltpu.VMEM((tm, d_model), jnp.float32)],
        compiler_params=_compiler_params(("parallel",)),
        name="ffn_down_ln2",
    )(hid, x1, wd, g2, b2)


def kernel(x, mem, w_in, rel_bias, sg_ln_g, sg_ln_b, w_spatial, b_spatial, w_mem_kv, w_out,
           ln1_g, ln1_b, w_gate, w_up, w_down, ln2_g, ln2_b):
    batch, seq, d_model = x.shape
    n_mem = mem.shape[1]
    depth = w_in.shape[0]
    h2d = x.reshape(batch * seq, d_model)
    for l in range(depth):
        hcat = _proj(h2d, w_in[l].astype(jnp.bfloat16), bm=1024, bn=PROJ_BN)
        kv = _kv_proj(mem.reshape(batch * n_mem, d_model), w_mem_kv[l], bn=D_MEM_ATTN)
        o_dil, w_out_b = _dilated_attention(hcat, rel_bias, w_out[l], batch, seq)
        x1, x1b = _mix(h2d, o_dil.reshape(batch * seq, D_DIL), hcat,
                       kv.reshape(batch, n_mem, 2 * D_MEM_ATTN), w_spatial[l], b_spatial[l].T,
                       sg_ln_g[l][None], sg_ln_b[l][None], w_out_b,
                       ln1_g[l][None], ln1_b[l][None], seq, tm=512)
        hid, w_down_b = _ffn_hidden(x1b, w_gate[l], w_up[l], w_down[l], bm=2048, bn=512)
        h2d = _ffn_down(hid, x1, w_down_b, ln2_g[l][None], ln2_b[l][None], tm=512)
    return h2d.reshape(batch, seq, d_model)
```
